```python
import math
import jax, jax.numpy as jnp
from jax import lax
import numpy as np


D_MODEL = 1024
BATCH = 2
SEQ = 16384
DEPTH = 1
DEC_BATCH = 8
DEC_SEQ = 8192
PAST_LEN = 128

N_MEM = 256
RMS_EPS = 1e-6
ROPE_THETA = 10000.0

GLA_HEADS = 4
GLA_DK = 128
GLA_DV = 256
GLA_LOWRANK = 16
GLA_GATE_NORM = 16.0
GLA_CHUNK = 64
GLA_QK = GLA_HEADS * GLA_DK
GLA_V = GLA_HEADS * GLA_DV

DIFF_HEADS = 8
DIFF_DH = 64
DIFF_DV = 2 * DIFF_DH
DIFF_QK = DIFF_HEADS * 2 * DIFF_DH
DIFF_V = DIFF_HEADS * DIFF_DV
QUERY_BLOCK = 128

MEM_HEADS = 4
MEM_DH = 256
MEM_Q = MEM_HEADS * MEM_DH

N_BRANCH = 3
SPLIT_SIZES = (GLA_QK, GLA_QK, GLA_V, GLA_V, GLA_LOWRANK, GLA_LOWRANK,
               DIFF_QK, DIFF_QK, DIFF_V, MEM_Q, N_BRANCH * D_MODEL)
IN_COLS = sum(SPLIT_SIZES)

PEER_HEADS = 8
PEER_NKEYS = 128
PEER_EXPERTS = PEER_NKEYS * PEER_NKEYS
PEER_DQ = 256
PEER_DQH = PEER_DQ // 2
PEER_TOPK = 16
PEER_TOKEN_BLOCK = 128

kernel_name = 'hybrid_gla_diffattn_peer_encoder'


def rmsnorm(x, g):
    xf = x.astype(jnp.float32)
    return xf * lax.rsqrt(jnp.mean(xf * xf, axis=-1, keepdims=True) + RMS_EPS) * g.astype(jnp.float32)


def rope(x):
    T, d = x.shape[1], x.shape[-1]
    inv_freq = ROPE_THETA ** (-jnp.arange(0, d, 2, dtype=jnp.float32) / d)
    ang = jnp.arange(T, dtype=jnp.float32)[:, None] * inv_freq[None, :]
    bshape = (1, T) + (1,) * (x.ndim - 3) + (d // 2,)
    cos, sin = jnp.cos(ang).reshape(bshape), jnp.sin(ang).reshape(bshape)
    x1, x2 = x[..., : d // 2], x[..., d // 2:]
    return jnp.concatenate([x1 * cos - x2 * sin, x2 * cos + x1 * sin], axis=-1)


def split_cols(proj):
    points = np.cumsum(np.array(SPLIT_SIZES))[:-1].tolist()
    return jnp.split(proj, points, axis=-1)


def gla_chunked(q, k, v, log_a):
    B, H, T, dk = q.shape
    dv = v.shape[-1]
    C = GLA_CHUNK
    N = T // C
    q = q.reshape(B, H, N, C, dk)
    k = k.reshape(B, H, N, C, dk)
    log_a = log_a.reshape(B, H, N, C, dk)
    v = v.reshape(B, H, N, C, dv)
    b = jnp.cumsum(log_a, axis=3)
    b_last = b[:, :, :, -1:, :]
    q_e = q * jnp.exp(b)
    k_e = k * jnp.exp(-b)
    k_s = k * jnp.exp(b_last - b)
    mask = jnp.tril(jnp.ones((C, C), dtype=bool))
    att = jnp.where(mask, jnp.einsum('bhncd,bhnsd->bhncs', q_e, k_e), 0.0)
    o_intra = jnp.einsum('bhncs,bhnse->bhnce', att, v)
    u = jnp.einsum('bhncd,bhnce->bhnde', k_s, v)
    g = jnp.exp(b_last[:, :, :, 0, :])

    def step(S, inp):
        g_n, u_n = inp
        return g_n[..., None] * S + u_n, S

    S0 = jnp.zeros((B, H, dk, dv), jnp.float32)
    _, S_prev = lax.scan(step, S0, (jnp.moveaxis(g, 2, 0), jnp.moveaxis(u, 2, 0)))
    S_prev = jnp.moveaxis(S_prev, 0, 2)
    o_inter = jnp.einsum('bhncd,bhnde->bhnce', q_e, S_prev)
    return (o_intra + o_inter).reshape(B, H, T, dv)


def gla_branch(q, k, v, gate, lr_f, lr_b, w_dec_f, b_dec_f, w_dec_b, b_dec_b, g_out, w_o):
    B, T, _ = q.shape
    dt = q.dtype

    def heads(t, d):
        return t.astype(jnp.float32).reshape(B, T, GLA_HEADS, d).transpose(0, 2, 1, 3)

    qh = heads(q, GLA_DK) * (GLA_DK ** -0.5)
    kh = heads(k, GLA_DK)
    vh = heads(v, GLA_DV)
    la_f = heads(jax.nn.log_sigmoid((lr_f @ w_dec_f + b_dec_f).astype(jnp.float32)) / GLA_GATE_NORM, GLA_DK)
    la_b = heads(jax.nn.log_sigmoid((lr_b @ w_dec_b + b_dec_b).astype(jnp.float32)) / GLA_GATE_NORM, GLA_DK)
    fwd = gla_chunked(qh, kh, vh, la_f)
    fl = lambda t: jnp.flip(t, axis=2)
    bwd = fl(gla_chunked(fl(qh), fl(kh), fl(vh), fl(la_b)))
    diag = jnp.sum(qh * kh, axis=-1, keepdims=True) * vh
    o = (fwd + bwd - diag).transpose(0, 2, 1, 3)
    o = rmsnorm(o, g_out) * jax.nn.silu(gate.astype(jnp.float32).reshape(B, T, GLA_HEADS, GLA_DV))
    return o.reshape(B, T, GLA_V).astype(dt) @ w_o


def diff_branch(q, k, v, g_q, g_k, lq1, lk1, lq2, lk2, g_sub, w_o, lam_init):
    B, T, _ = q.shape
    dt = q.dtype
    qh = rope(rmsnorm(q.reshape(B, T, DIFF_HEADS, 2, DIFF_DH), g_q))
    kh = rope(rmsnorm(k.reshape(B, T, DIFF_HEADS, 2, DIFF_DH), g_k))
    vh = v.astype(jnp.float32).reshape(B, T, DIFF_HEADS, DIFF_DV)
    f32 = lambda t: t.astype(jnp.float32)
    lam = jnp.exp(jnp.sum(f32(lq1) * f32(lk1))) - jnp.exp(jnp.sum(f32(lq2) * f32(lk2))) + lam_init
    scale = DIFF_DH ** -0.5
    nb = T // QUERY_BLOCK
    qb = qh.reshape(B, nb, QUERY_BLOCK, DIFF_HEADS, 2, DIFF_DH).swapaxes(0, 1)

    def block(qi):
        s = jnp.einsum('bqhcd,bkhcd->bhcqk', qi, kh) * scale
        p = jax.nn.softmax(s, axis=-1)
        a = p[:, :, 0] - lam * p[:, :, 1]
        return jnp.einsum('bhqk,bkhe->bqhe', a, vh)

    o = lax.map(block, qb).swapaxes(0, 1).reshape(B, T, DIFF_HEADS, DIFF_DV)
    o = rmsnorm(o, g_sub) * (1.0 - lam_init)
    return o.reshape(B, T, DIFF_V).astype(dt) @ w_o


def mem_branch(q, mem, g_mem, w_kv, g_q, g_k, w_o):
    B, T, _ = q.shape
    M = mem.shape[1]
    dt = q.dtype
    kv = rmsnorm(mem, g_mem).astype(dt) @ w_kv
    k, v = jnp.split(kv, 2, axis=-1)
    kh = rmsnorm(k.reshape(B, M, MEM_HEADS, MEM_DH), g_k)
    vh = v.astype(jnp.float32).reshape(B, M, MEM_HEADS, MEM_DH)
    qh = rmsnorm(q.reshape(B, T, MEM_HEADS, MEM_DH), g_q)
    s = jnp.einsum('bthd,bmhd->bhtm', qh, kh) * (MEM_DH ** -0.5)
    p = jax.nn.softmax(s, axis=-1)
    o = jnp.einsum('bhtm,bmhd->bthd', p, vh)
    return o.reshape(B, T, MEM_Q).astype(dt) @ w_o


def peer(h, w_q, sub_k1, sub_k2, u_tab, v_tab):
    B, T, D = h.shape
    tokens = h.reshape(-1, PEER_TOKEN_BLOCK, D)
    k1 = sub_k1.astype(jnp.float32)
    k2 = sub_k2.astype(jnp.float32)

    def block(hb):
        n = hb.shape[0]
        q = (hb @ w_q).astype(jnp.float32).reshape(n, PEER_HEADS, 2, PEER_DQH)
        s1 = jnp.einsum('nhd,kd->nhk', q[:, :, 0], k1)
        s2 = jnp.einsum('nhd,kd->nhk', q[:, :, 1], k2)
        v1, i1 = lax.top_k(s1, PEER_TOPK)
        v2, i2 = lax.top_k(s2, PEER_TOPK)
        cand = (v1[..., :, None] + v2[..., None, :]).reshape(n, PEER_HEADS, PEER_TOPK * PEER_TOPK)
        cidx = (i1[..., :, None] * PEER_NKEYS + i2[..., None, :]).reshape(n, PEER_HEADS, PEER_TOPK * PEER_TOPK)
        top_s, pos = lax.top_k(cand, PEER_TOPK)
        eidx = jnp.take_along_axis(cidx, pos, axis=-1)
        gates = jax.nn.softmax(top_s, axis=-1)
        a = jnp.einsum('nhkd,nd->nhk', u_tab[eidx].astype(jnp.float32), hb.astype(jnp.float32))
        w = gates * jax.nn.gelu(a, approximate=False)
        return jnp.einsum('nhk,nhkd->nd', w, v_tab[eidx].astype(jnp.float32)).astype(h.dtype)

    return lax.map(block, tokens).reshape(B, T, D)


def trunk(x, mem, g_mix, w_in, gla_w_dec_f, gla_b_dec_f, gla_w_dec_b, gla_b_dec_b, gla_g_out, gla_w_o,
          diff_g_q, diff_g_k, diff_lq1, diff_lk1, diff_lq2, diff_lk2, diff_g_sub, diff_w_o,
          mem_g_norm, mem_w_kv, mem_g_q, mem_g_k, mem_w_o, w_out,
          g_ffn, peer_w_q, peer_sub_k1, peer_sub_k2, peer_u, peer_v):
    dt = x.dtype
    B, T, D = x.shape
    for l in range(DEPTH):
        lam_init = 0.8 - 0.6 * math.exp(-0.3 * l)
        h = rmsnorm(x, g_mix[l]).astype(dt)
        (g_q, g_k, g_v, g_gate, lr_f, lr_b, d_q, d_k, d_v, m_q, br_gate) = split_cols(h @ w_in[l])
        y_gla = gla_branch(g_q, g_k, g_v, g_gate, lr_f, lr_b, gla_w_dec_f[l], gla_b_dec_f[l],
                           gla_w_dec_b[l], gla_b_dec_b[l], gla_g_out[l], gla_w_o[l])
        y_diff = diff_branch(d_q, d_k, d_v, diff_g_q[l], diff_g_k[l], diff_lq1[l], diff_lk1[l],
                             diff_lq2[l], diff_lk2[l], diff_g_sub[l], diff_w_o[l], lam_init)
        y_mem = mem_branch(m_q, mem, mem_g_norm[l], mem_w_kv[l], mem_g_q[l], mem_g_k[l], mem_w_o[l])
        gates = jax.nn.sigmoid(br_gate.astype(jnp.float32)).reshape(B, T, N_BRANCH, D)
        merged = (gates[:, :, 0] * y_gla.astype(jnp.float32) + gates[:, :, 1] * y_diff.astype(jnp.float32)
                  + gates[:, :, 2] * y_mem.astype(jnp.float32))
        x = x + merged.astype(dt) @ w_out[l]
        x = x + peer(rmsnorm(x, g_ffn[l]).astype(dt), peer_w_q[l], peer_sub_k1[l], peer_sub_k2[l],
                     peer_u[l], peer_v[l])
    return x


def setup_inputs(seed: int = 0) -> dict:
    key = jax.random.key(seed)
    ks = iter(jax.random.split(key, 40))

    def nrm(shape, scale):
        return jax.random.normal(next(ks), shape, jnp.float32) * scale

    def gain(d):
        return 1.0 + nrm((DEPTH, d), 0.05)

    L, D = DEPTH, D_MODEL
    return {
        'x_prompt': nrm((BATCH, SEQ, D), 1.0),
        'x_sample': nrm((DEC_BATCH, DEC_SEQ, D), 1.0),
        'mem_prompt': nrm((BATCH, N_MEM, D), 1.0),
        'mem_sample': nrm((DEC_BATCH, N_MEM, D), 1.0),
        'g_mix': gain(D),
        'w_in': nrm((L, D, IN_COLS), D ** -0.5),
        'gla_w_dec_f': nrm((L, GLA_LOWRANK, GLA_QK), GLA_LOWRANK ** -0.5),
        'gla_b_dec_f': nrm((L, GLA_QK), 0.1),
        'gla_w_dec_b': nrm((L, GLA_LOWRANK, GLA_QK), GLA_LOWRANK ** -0.5),
        'gla_b_dec_b': nrm((L, GLA_QK), 0.1),
        'gla_g_out': gain(GLA_DV),
        'gla_w_o': nrm((L, GLA_V, D), GLA_V ** -0.5),
        'diff_g_q': gain(DIFF_DH),
        'diff_g_k': gain(DIFF_DH),
        'diff_lq1': nrm((L, DIFF_DH), 0.1),
        'diff_lk1': nrm((L, DIFF_DH), 0.1),
        'diff_lq2': nrm((L, DIFF_DH), 0.1),
        'diff_lk2': nrm((L, DIFF_DH), 0.1),
        'diff_g_sub': gain(DIFF_DV),
        'diff_w_o': nrm((L, DIFF_V, D), DIFF_V ** -0.5),
        'mem_g_norm': gain(D),
        'mem_w_kv': nrm((L, D, 2 * MEM_Q), D ** -0.5),
        'mem_g_q': gain(MEM_DH),
        'mem_g_k': gain(MEM_DH),
        'mem_w_o': nrm((L, MEM_Q, D), MEM_Q ** -0.5),
        'w_out': nrm((L, D, D), D ** -0.5),
        'g_ffn': gain(D),
        'peer_w_q': nrm((L, D, PEER_HEADS * PEER_DQ), D ** -0.5),
        'peer_sub_k1': nrm((L, PEER_NKEYS, PEER_DQH), PEER_DQH ** -0.5),
        'peer_sub_k2': nrm((L, PEER_NKEYS, PEER_DQH), PEER_DQH ** -0.5),
        'peer_u': nrm((L, PEER_EXPERTS, D), D ** -0.5),
        'peer_v': nrm((L, PEER_EXPERTS, D), (PEER_HEADS * PEER_TOPK) ** -0.5),
    }


def reference(x_prompt, x_sample, mem_prompt, mem_sample, g_mix, w_in, gla_w_dec_f, gla_b_dec_f,
              gla_w_dec_b, gla_b_dec_b, gla_g_out, gla_w_o, diff_g_q, diff_g_k, diff_lq1, diff_lk1,
              diff_lq2, diff_lk2, diff_g_sub, diff_w_o, mem_g_norm, mem_w_kv, mem_g_q, mem_g_k, mem_w_o,
              w_out, g_ffn, peer_w_q, peer_sub_k1, peer_sub_k2, peer_u, peer_v):
    weights = (g_mix, w_in, gla_w_dec_f, gla_b_dec_f, gla_w_dec_b, gla_b_dec_b, gla_g_out, gla_w_o,
               diff_g_q, diff_g_k, diff_lq1, diff_lk1, diff_lq2, diff_lk2, diff_g_sub, diff_w_o,
               mem_g_norm, mem_w_kv, mem_g_q, mem_g_k, mem_w_o, w_out,
               g_ffn, peer_w_q, peer_sub_k1, peer_sub_k2, peer_u, peer_v)
    y_prompt = trunk(x_prompt, mem_prompt, *weights)
    y_sample = trunk(x_sample, mem_sample, *weights)
    return (y_prompt, y_sample)
```

```python
import functools
import math

import jax
import jax.numpy as jnp
from jax import lax
from jax.experimental import pallas as pl
from jax.experimental.pallas import tpu as pltpu

F32 = jnp.float32
BF16 = jnp.bfloat16

D_MODEL = 1024
N_MEM = 256
RMS_EPS = 1e-6
ROPE_THETA = 10000.0

GLA_HEADS = 4
GLA_DK = 128
GLA_DV = 256
GLA_LOWRANK = 16
GLA_GATE_NORM = 16.0
GLA_QK = GLA_HEADS * GLA_DK
GLA_V = GLA_HEADS * GLA_DV

DIFF_HEADS = 8
DIFF_DH = 64
DIFF_DV = 2 * DIFF_DH
DIFF_QK = DIFF_HEADS * 2 * DIFF_DH
DIFF_V = DIFF_HEADS * DIFF_DV

MEM_HEADS = 4
MEM_DH = 256
MEM_Q = MEM_HEADS * MEM_DH

N_BRANCH = 3
PEER_HEADS = 8
PEER_NKEYS = 128
PEER_EXPERTS = PEER_NKEYS * PEER_NKEYS
PEER_DQ = 256
PEER_DQH = PEER_DQ // 2
PEER_TOPK = 16

LANES = 128
SUBLANES = 8
VMEM_LIMIT = 56 * 1024 * 1024

OFF_GQ = 0
OFF_GK = OFF_GQ + GLA_QK
OFF_GV = OFF_GK + GLA_QK
OFF_GG = OFF_GV + GLA_V
OFF_DQ = OFF_GG + GLA_V
OFF_DK = OFF_DQ + DIFF_QK
OFF_DV = OFF_DK + DIFF_QK
OFF_MQ = OFF_DV + DIFF_V
OFF_BR = OFF_MQ + MEM_Q
MAIN_COLS = OFF_BR + N_BRANCH * D_MODEL

LOG2E = 1.4426950408889634


def _dot(a, b):
    return jnp.dot(a, b, preferred_element_type=F32)


def _dot_nt(a, b):
    return lax.dot_general(a, b, (((1,), (1,)), ((), ())), preferred_element_type=F32)


def _params(*sem):
    return pltpu.CompilerParams(dimension_semantics=sem, vmem_limit_bytes=VMEM_LIMIT)


def _rms_matmul_kernel(x_ref, g_ref, w_ref, o_ref, h_ref):
    @pl.when(pl.program_id(1) == 0)
    def _():
        x = x_ref[...]
        ms = jnp.mean(x * x, axis=-1, keepdims=True)
        h_ref[...] = (x * lax.rsqrt(ms + RMS_EPS) * g_ref[...]).astype(BF16)

    o_ref[...] = _dot(h_ref[...], w_ref[...]).astype(o_ref.dtype)


def _rms_matmul(x, g, w, out_dtype, tm, tn):
    n, d = x.shape
    m = w.shape[1]
    tm = min(tm, n)
    tn = min(tn, m)
    return pl.pallas_call(
        _rms_matmul_kernel,
        grid=(n // tm, m // tn),
        in_specs=[pl.BlockSpec((tm, d), lambda i, j: (i, 0)),
                  pl.BlockSpec((1, d), lambda i, j: (0, 0)),
                  pl.BlockSpec((d, tn), lambda i, j: (0, j))],
        out_specs=pl.BlockSpec((tm, tn), lambda i, j: (i, j)),
        out_shape=jax.ShapeDtypeStruct((n, m), out_dtype),
        scratch_shapes=[pltpu.VMEM((tm, d), BF16)],
        compiler_params=_params("parallel", "arbitrary"),
    )(x, g.reshape(1, d), w)


GLA_CHUNK = 128


def _gla_kernel(*refs, reverse, nchunk, final):
    if final:
        (q_ref, k_ref, v_ref, lr_ref, wd_ref, bd_ref, of_ref, gate_ref, gout_ref, o_ref, st_ref) = refs
    else:
        (q_ref, k_ref, v_ref, lr_ref, wd_ref, bd_ref, o_ref, st_ref) = refs
    C = GLA_CHUNK

    @pl.when(pl.program_id(1) == 0)
    def _():
        st_ref[...] = jnp.zeros_like(st_ref)

    rows = lax.broadcasted_iota(jnp.int32, (C, C), 0)
    cols = lax.broadcasted_iota(jnp.int32, (C, C), 1)
    tri = jnp.where(cols <= rows, 1.0, 0.0).astype(BF16)
    keep = (cols >= rows) if reverse else (cols <= rows)
    scale = GLA_DK ** -0.5
    order = range(nchunk - 1, -1, -1) if reverse else range(nchunk)
    for ci in order:
        sl = pl.ds(ci * C, C)
        lr_c = lr_ref[0, sl, :].astype(BF16)
        for h in range(GLA_HEADS):
            dk = slice(h * GLA_DK, (h + 1) * GLA_DK)
            dv = slice(h * GLA_DV, (h + 1) * GLA_DV)
            q = q_ref[0, sl, dk].astype(F32) * scale
            k = k_ref[0, sl, dk].astype(F32)
            v = v_ref[0, sl, dv]
            pre = _dot(lr_c, wd_ref[h]) + bd_ref[h]
            la = (jnp.minimum(pre, 0.0) - jnp.log1p(jnp.exp(-jnp.abs(pre)))) * (1.0 / GLA_GATE_NORM)
            la_hi = la.astype(BF16)
            la_lo = (la - la_hi.astype(F32)).astype(BF16)
            b = _dot(tri, la_hi) + _dot(tri, la_lo)
            tot = b[C - 1:C, :]
            st = st_ref[h]
            vt = v.astype(F32).T.astype(BF16)
            if not reverse:
                q_in = (q * jnp.exp(b)).astype(BF16)
                q_st = q_in
                k_in = (k * jnp.exp(-b)).astype(BF16)
                k_st = (k * jnp.exp(tot - b)).astype(BF16)
            else:
                c = b - la
                q_in = (q * jnp.exp(-c)).astype(BF16)
                q_st = (q * jnp.exp(tot - c)).astype(BF16)
                k_in = (k * jnp.exp(c)).astype(BF16)
                k_st = k_in
            att = jnp.where(keep, _dot_nt(q_in, k_in), 0.0)
            o = _dot(att.astype(BF16), v) + _dot_nt(q_st, st.astype(BF16))
            st_ref[h] = jnp.exp(tot) * st + _dot(vt, k_st)
            if final:
                diag = jnp.sum(q * k, axis=-1, keepdims=True)
                o = of_ref[0, sl, dv] + o - diag * v.astype(F32)
                ms = jnp.mean(o * o, axis=-1, keepdims=True)
                o = o * lax.rsqrt(ms + RMS_EPS) * gout_ref[...]
                gt = gate_ref[0, sl, dv].astype(F32)
                o = o * (gt * jax.nn.sigmoid(gt))
            o_ref[0, sl, dv] = o.astype(o_ref.dtype)


def _gla_branch(proj, lr, w_dec_f, b_dec_f, w_dec_b, b_dec_b, g_out, tb):
    bsz, t, _ = proj.shape
    tb = min(tb, t)
    nblk = t // tb
    nchunk = tb // GLA_CHUNK

    def dec_weights(w_dec, b_dec, row0):
        w = jnp.zeros((GLA_HEADS, LANES, GLA_DK), F32)
        w = w.at[:, row0:row0 + GLA_LOWRANK, :].set(
            w_dec.reshape(GLA_LOWRANK, GLA_HEADS, GLA_DK).transpose(1, 0, 2))
        return w.astype(BF16), b_dec.reshape(GLA_HEADS, 1, GLA_DK).astype(F32)

    def call(reverse, o_fwd):
        wd, bd = dec_weights(w_dec_b, b_dec_b, GLA_LOWRANK) if reverse else dec_weights(w_dec_f, b_dec_f, 0)
        blk = (lambda j: nblk - 1 - j) if reverse else (lambda j: j)
        in_specs = [
            pl.BlockSpec((1, tb, GLA_QK), lambda b, j: (b, blk(j), OFF_GQ // GLA_QK)),
            pl.BlockSpec((1, tb, GLA_QK), lambda b, j: (b, blk(j), OFF_GK // GLA_QK)),
            pl.BlockSpec((1, tb, GLA_V), lambda b, j: (b, blk(j), OFF_GV // GLA_V)),
            pl.BlockSpec((1, tb, LANES), lambda b, j: (b, blk(j), 0)),
            pl.BlockSpec((GLA_HEADS, LANES, GLA_DK), lambda b, j: (0, 0, 0)),
            pl.BlockSpec((GLA_HEADS, 1, GLA_DK), lambda b, j: (0, 0, 0)),
        ]
        args = [proj, proj, proj, lr, wd, bd]
        if reverse:
            in_specs += [
                pl.BlockSpec((1, tb, GLA_V), lambda b, j: (b, blk(j), 0)),
                pl.BlockSpec((1, tb, GLA_V), lambda b, j: (b, blk(j), OFF_GG // GLA_V)),
                pl.BlockSpec((1, GLA_DV), lambda b, j: (0, 0)),
            ]
            args += [o_fwd, proj, g_out.reshape(1, GLA_DV).astype(F32)]
        return pl.pallas_call(
            functools.partial(_gla_kernel, reverse=reverse, nchunk=nchunk, final=reverse),
            grid=(bsz, nblk),
            in_specs=in_specs,
            out_specs=pl.BlockSpec((1, tb, GLA_V), lambda b, j: (b, blk(j), 0)),
            out_shape=jax.ShapeDtypeStruct((bsz, t, GLA_V), BF16 if reverse else F32),
            scratch_shapes=[pltpu.VMEM((GLA_HEADS, GLA_DV, GLA_DK), F32)],
            compiler_params=_params("parallel", "arbitrary"),
        )(*args)

    return call(True, call(False, None))


def _group_sumsq(x, ones_bd):
    x2 = x * x
    hi = x2.astype(BF16)
    lo = (x2 - hi.astype(F32)).astype(BF16)
    pieces = []
    for c in range(x.shape[1] // LANES):
        cs = slice(c * LANES, (c + 1) * LANES)
        pieces.append(_dot(hi[:, cs], ones_bd) + _dot(lo[:, cs], ones_bd))
    return jnp.concatenate(pieces, axis=1)


def _diff_prep_kernel(q_ref, k_ref, gq_ref, gk_ref, cos_ref, sin_ref, qz_ref, kt_ref):
    tm = q_ref.shape[1]
    r = lax.broadcasted_iota(jnp.int32, (LANES, LANES), 0) // DIFF_DH
    c = lax.broadcasted_iota(jnp.int32, (LANES, LANES), 1) // DIFF_DH
    ones_bd = jnp.where(r == c, 1.0, 0.0).astype(BF16)
    lane = lax.broadcasted_iota(jnp.int32, (tm, DIFF_QK), 1)
    first_half = (lane % DIFF_DH) < (DIFF_DH // 2)
    reps = DIFF_QK // LANES
    cos = jnp.concatenate([cos_ref[...]] * reps, axis=1)
    sin = jnp.concatenate([sin_ref[...]] * reps, axis=1)

    def norm_rope(x, g):
        ms = _group_sumsq(x, ones_bd) * (1.0 / DIFF_DH)
        xn = x * lax.rsqrt(ms + RMS_EPS) * g
        partner = jnp.where(first_half,
                            pltpu.roll(xn, DIFF_QK - DIFF_DH // 2, axis=1),
                            pltpu.roll(xn, DIFF_DH // 2, axis=1))
        return xn * cos + partner * sin

    qr = norm_rope(q_ref[0].astype(F32), gq_ref[...]) * (DIFF_DH ** -0.5 * LOG2E)
    kr = norm_rope(k_ref[0].astype(F32), gk_ref[...])
    lane_h = lax.broadcasted_iota(jnp.int32, (tm, LANES), 1)
    pieces = []
    for h in range(DIFF_HEADS):
        qh = qr[:, h * LANES:(h + 1) * LANES]
        pieces.append(jnp.where(lane_h < DIFF_DH, qh, 0.0))
        pieces.append(jnp.where(lane_h >= DIFF_DH, qh, 0.0))
        kt_ref[0, h] = kr[:, h * LANES:(h + 1) * LANES].T.astype(BF16)
    qz_ref[0] = jnp.concatenate(pieces, axis=1).astype(BF16)


def _flash_kernel(qz_ref, kt_ref, v_ref, lq1_ref, lk1_ref, lq2_ref, lk2_ref, gsub_ref, o_ref,
                  qs_ref, m_ref, l_ref, acc_ref, *, tq, tk, nk, lam_init):
    qs_ref[0:tq, :] = qz_ref[0, :, 0:LANES]
    qs_ref[tq:2 * tq, :] = qz_ref[0, :, LANES:2 * LANES]
    m_ref[...] = jnp.full_like(m_ref, -jnp.inf)
    l_ref[...] = jnp.zeros_like(l_ref)
    acc_ref[...] = jnp.zeros_like(acc_ref)
    reps = tk // LANES

    def step(i, carry):
        k0 = pl.multiple_of(i * tk, tk)
        s = _dot(qs_ref[...], kt_ref[0, 0, :, pl.ds(k0, tk)])
        m_prev = m_ref[...]
        m_new = jnp.maximum(m_prev, jnp.max(s, axis=1, keepdims=True))
        alpha = jnp.exp2(m_prev - m_new)
        p = jnp.exp2(s - jnp.concatenate([m_new] * reps, axis=1))
        l_ref[...] = alpha * l_ref[...] + jnp.sum(p, axis=1, keepdims=True)
        acc_ref[...] = alpha * acc_ref[...] + _dot(p.astype(BF16), v_ref[0, pl.ds(k0, tk), :])
        m_ref[...] = m_new
        return carry

    lax.fori_loop(0, nk, step, 0)

    o = acc_ref[...] / l_ref[...]
    lam = (jnp.exp(jnp.sum(lq1_ref[...] * lk1_ref[...], axis=1, keepdims=True))
           - jnp.exp(jnp.sum(lq2_ref[...] * lk2_ref[...], axis=1, keepdims=True)) + lam_init)
    o = o[0:tq] - lam * o[tq:2 * tq]
    ms = jnp.mean(o * o, axis=-1, keepdims=True)
    o = o * lax.rsqrt(ms + RMS_EPS) * gsub_ref[...] * (1.0 - lam_init)
    o_ref[0] = o.astype(o_ref.dtype)


def _rope_tables(t):
    half = DIFF_DH // 2
    inv_freq = ROPE_THETA ** (-jnp.arange(0, DIFF_DH, 2, dtype=F32) / DIFF_DH)
    ang = jnp.arange(t, dtype=F32)[:, None] * inv_freq[None, :]
    cos, sin = jnp.cos(ang), jnp.sin(ang)
    cos_t = jnp.tile(cos, (1, LANES // half))
    sin_t = jnp.tile(jnp.concatenate([-sin, sin], axis=1), (1, LANES // DIFF_DH))
    return cos_t, sin_t


def _diff_branch(proj, g_q, g_k, lq1, lk1, lq2, lk2, g_sub, lam_init, tm, tq, tk):
    bsz, t, _ = proj.shape
    tm, tq, tk = min(tm, t), min(tq, t), min(tk, t)
    cos_t, sin_t = _rope_tables(t)
    tile_g = lambda g: jnp.tile(g.astype(F32), DIFF_QK // DIFF_DH).reshape(1, DIFF_QK)
    qz, kt = pl.pallas_call(
        _diff_prep_kernel,
        grid=(bsz, t // tm),
        in_specs=[pl.BlockSpec((1, tm, DIFF_QK), lambda b, i: (b, i, OFF_DQ // DIFF_QK)),
                  pl.BlockSpec((1, tm, DIFF_QK), lambda b, i: (b, i, OFF_DK // DIFF_QK)),
                  pl.BlockSpec((1, DIFF_QK), lambda b, i: (0, 0)),
                  pl.BlockSpec((1, DIFF_QK), lambda b, i: (0, 0)),
                  pl.BlockSpec((tm, LANES), lambda b, i: (i, 0)),
                  pl.BlockSpec((tm, LANES), lambda b, i: (i, 0))],
        out_specs=[pl.BlockSpec((1, tm, 2 * DIFF_QK), lambda b, i: (b, i, 0)),
                   pl.BlockSpec((1, DIFF_HEADS, LANES, tm), lambda b, i: (b, 0, 0, i))],
        out_shape=[jax.ShapeDtypeStruct((bsz, t, 2 * DIFF_QK), BF16),
                   jax.ShapeDtypeStruct((bsz, DIFF_HEADS, LANES, t), BF16)],
        compiler_params=_params("parallel", "parallel"),
    )(proj, proj, tile_g(g_q), tile_g(g_k), cos_t, sin_t)

    vec = lambda a: a.reshape(1, DIFF_DH).astype(F32)
    return pl.pallas_call(
        functools.partial(_flash_kernel, tq=tq, tk=tk, nk=t // tk, lam_init=lam_init),
        grid=(bsz, DIFF_HEADS, t // tq),
        in_specs=[pl.BlockSpec((1, tq, 2 * LANES), lambda b, h, i: (b, i, h)),
                  pl.BlockSpec((1, 1, LANES, t), lambda b, h, i: (b, h, 0, 0)),
                  pl.BlockSpec((1, t, DIFF_DV), lambda b, h, i: (b, 0, OFF_DV // DIFF_DV + h)),
                  pl.BlockSpec((1, DIFF_DH), lambda b, h, i: (0, 0)),
                  pl.BlockSpec((1, DIFF_DH), lambda b, h, i: (0, 0)),
                  pl.BlockSpec((1, DIFF_DH), lambda b, h, i: (0, 0)),
                  pl.BlockSpec((1, DIFF_DH), lambda b, h, i: (0, 0)),
                  pl.BlockSpec((1, DIFF_DV), lambda b, h, i: (0, 0))],
        out_specs=pl.BlockSpec((1, tq, DIFF_DV), lambda b, h, i: (b, i, h)),
        out_shape=jax.ShapeDtypeStruct((bsz, t, DIFF_V), BF16),
        scratch_shapes=[pltpu.VMEM((2 * tq, LANES), BF16),
                        pltpu.VMEM((2 * tq, LANES), F32),
                        pltpu.VMEM((2 * tq, LANES), F32),
                        pltpu.VMEM((2 * tq, DIFF_DV), F32)],
        compiler_params=_params("parallel", "parallel", "arbitrary"),
    )(qz, kt, proj, vec(lq1), vec(lk1), vec(lq2), vec(lk2), g_sub.reshape(1, DIFF_DV).astype(F32))


def _mem_kernel(q_ref, kv_ref, gq_ref, gk_ref, o_ref):
    outs = []
    for h in range(MEM_HEADS):
        hs = slice(h * MEM_DH, (h + 1) * MEM_DH)
        q = q_ref[0, :, hs].astype(F32)
        q = q * lax.rsqrt(jnp.mean(q * q, axis=-1, keepdims=True) + RMS_EPS) * gq_ref[...]
        k = kv_ref[0, :, hs]
        k = k * lax.rsqrt(jnp.mean(k * k, axis=-1, keepdims=True) + RMS_EPS) * gk_ref[...]
        v = kv_ref[0, :, MEM_Q + h * MEM_DH:MEM_Q + (h + 1) * MEM_DH]
        s = _dot_nt(q.astype(BF16), k.astype(BF16)) * (MEM_DH ** -0.5)
        p = jnp.exp(s - jnp.max(s, axis=-1, keepdims=True))
        p = p / jnp.sum(p, axis=-1, keepdims=True)
        outs.append(_dot(p.astype(BF16), v.astype(BF16)))
    o_ref[0] = jnp.concatenate(outs, axis=1).astype(o_ref.dtype)


def _mem_branch(proj, kv, g_q, g_k, tm):
    bsz, t, _ = proj.shape
    tm = min(tm, t)
    return pl.pallas_call(
        _mem_kernel,
        grid=(bsz, t // tm),
        in_specs=[pl.BlockSpec((1, tm, MEM_Q), lambda b, i: (b, i, OFF_MQ // MEM_Q)),
                  pl.BlockSpec((1, N_MEM, 2 * MEM_Q), lambda b, i: (b, 0, 0)),
                  pl.BlockSpec((1, MEM_DH), lambda b, i: (0, 0)),
                  pl.BlockSpec((1, MEM_DH), lambda b, i: (0, 0))],
        out_specs=pl.BlockSpec((1, tm, MEM_Q), lambda b, i: (b, i, 0)),
        out_shape=jax.ShapeDtypeStruct((bsz, t, MEM_Q), BF16),
        compiler_params=_params("parallel", "parallel"),
    )(proj, kv, g_q.reshape(1, MEM_DH).astype(F32), g_k.reshape(1, MEM_DH).astype(F32))


def _merge_kernel(x_ref, pg_ref, pd_ref, pm_ref, g0_ref, g1_ref, g2_ref,
                  wg_ref, wd_ref, wm_ref, wo_ref, o_ref):
    merged = (jax.nn.sigmoid(g0_ref[...].astype(F32)) * _dot(pg_ref[...], wg_ref[...])
              + jax.nn.sigmoid(g1_ref[...].astype(F32)) * _dot(pd_ref[...], wd_ref[...])
              + jax.nn.sigmoid(g2_ref[...].astype(F32)) * _dot(pm_ref[...], wm_ref[...]))
    o_ref[...] = x_ref[...] + _dot(merged.astype(BF16), wo_ref[...])


def _merge(x, proj, pg, pd, pm, wg, wd, wm, wo, tm):
    n, d = x.shape
    tm = min(tm, n)
    row = lambda c: pl.BlockSpec((tm, d), lambda i: (i, c))
    full = pl.BlockSpec((d, d), lambda i: (0, 0))
    br = OFF_BR // d
    return pl.pallas_call(
        _merge_kernel,
        grid=(n // tm,),
        in_specs=[row(0), row(0), row(0), row(0), row(br), row(br + 1), row(br + 2), full, full, full, full],
        out_specs=row(0),
        out_shape=jax.ShapeDtypeStruct((n, d), F32),
        compiler_params=_params("parallel"),
    )(x, pg, pd, pm, proj, proj, proj, wg, wd, wm, wo)


PEER_ECHUNK = 1024
PEER_IPER = PEER_ECHUNK // PEER_NKEYS


def _bitonic_merge_desc(a):
    n = len(a)
    j = n // 2
    while j >= 1:
        for i in range(n):
            l = i ^ j
            if l > i:
                a[i], a[l] = jnp.maximum(a[i], a[l]), jnp.minimum(a[i], a[l])
        j //= 2
    return a


def _top16_desc(vals):
    a = list(vals)
    n = len(a)
    k = 2
    while k <= n:
        j = k // 2
        while j >= 1:
            for i in range(n):
                l = i ^ j
                if l > i:
                    hi, lo = jnp.maximum(a[i], a[l]), jnp.minimum(a[i], a[l])
                    a[i], a[l] = (hi, lo) if (i & k) == 0 else (lo, hi)
            j //= 2
        k *= 2
    for shift in (4, 2, 1):
        a = [jnp.maximum(a[r], pltpu.roll(a[n - 1 - r], shift, axis=0)) for r in range(n)]
        a = _bitonic_merge_desc(a)
    return a


def _peer_kernel(x_ref, g_ref, wqt_ref, k1h_ref, k1l_ref, k2h_ref, k2l_ref, u_ref, vt_ref, o_ref,
                 hbt_ref, t_ref, c_ref, s2_ref, e2_ref, wt_ref, acc_ref):
    ec = pl.program_id(1)
    tt = x_ref.shape[0]
    neg = jnp.full((SUBLANES, tt), -jnp.inf, F32)
    sub = lax.broadcasted_iota(jnp.int32, (SUBLANES, tt), 0)

    @pl.when(ec == 0)
    def _():
        x = x_ref[...]
        hb = x * lax.rsqrt(jnp.mean(x * x, axis=-1, keepdims=True) + RMS_EPS) * g_ref[...]
        hbt_ref[...] = hb.T.astype(BF16)
        acc_ref[...] = jnp.zeros_like(acc_ref)
        for h in range(PEER_HEADS):
            def scores(kh_ref, kl_ref, row0):
                qt = _dot(wqt_ref[row0:row0 + PEER_DQH, :], hbt_ref[...])
                qh = qt.astype(BF16)
                ql = (qt - qh.astype(F32)).astype(BF16)
                return _dot(kh_ref[...], qh) + _dot(kh_ref[...], ql) + _dot(kl_ref[...], qh)

            s1 = scores(k1h_ref, k1l_ref, h * PEER_DQ)
            s2 = scores(k2h_ref, k2l_ref, h * PEER_DQ + PEER_DQH)
            blocks = lambda s: [s[r * SUBLANES:(r + 1) * SUBLANES, :] for r in range(PEER_NKEYS // SUBLANES)]
            v1 = _top16_desc(blocks(s1))
            v2 = _top16_desc(blocks(s2))

            def pack(vs):
                out = neg
                for r in range(SUBLANES):
                    out = jnp.where(sub == r, vs[r], out)
                return out

            v2_lo, v2_hi, v1_hi = pack(v2[:SUBLANES]), pack(v2[SUBLANES:]), pack(v1[SUBLANES:])
            cand = [v1[0] + v2_lo, v1[0] + v2_hi, v1_hi + v2[0]]
            cand += [v1[a] + v2_lo for a in range(1, SUBLANES)]
            cand += [neg] * (PEER_TOPK - len(cand))
            top = _top16_desc(cand)
            tau16 = top[PEER_TOPK - 1]
            below = lambda s, bound: jnp.max(jnp.where(s < bound, s, -jnp.inf), axis=0, keepdims=True)
            tau17 = below(jnp.concatenate(cand[:10], axis=0), tau16[0:1, :])
            tau17 = jnp.maximum(tau17, below(s1, v1[PEER_TOPK - 1][0:1, :]) + v2[0][0:1, :])
            tau17 = jnp.maximum(tau17, below(s2, v2[PEER_TOPK - 1][0:1, :]) + v1[0][0:1, :])
            tau = 0.5 * (tau16[0:1, :] + tau17)
            z = top[0] - top[0]
            for r in range(PEER_TOPK):
                z = z + jnp.exp(top[r] - top[0])
            z = z[0:1, :]
            t_ref[h] = tau - s1
            c_ref[h] = jnp.exp(s1 - v1[0][0:1, :]) / z
            s2_ref[h] = s2
            e2_ref[h] = jnp.exp(s2 - v2[0][0:1, :])

    a_t = _dot(u_ref[...], hbt_ref[...])
    for il in range(PEER_IPER):
        i = ec * PEER_IPER + il
        g = jnp.zeros((PEER_NKEYS, tt), F32)
        for h in range(PEER_HEADS):
            thr = t_ref[h, pl.ds(i, 1), :]
            coef = c_ref[h, pl.ds(i, 1), :]
            g = g + jnp.where(s2_ref[h] >= thr, e2_ref[h], 0.0) * coef
        a = a_t[il * PEER_NKEYS:(il + 1) * PEER_NKEYS, :]
        gelu = 0.5 * a * (1.0 + lax.erf(a * (2.0 ** -0.5)))
        wt_ref[il * PEER_NKEYS:(il + 1) * PEER_NKEYS, :] = (gelu * g).astype(BF16)
    acc_ref[...] += _dot(vt_ref[...], wt_ref[...])

    @pl.when(ec == pl.num_programs(1) - 1)
    def _():
        o_ref[...] = x_ref[...] + acc_ref[...].T


def _peer(x, g_ffn, wqt, k1h, k1l, k2h, k2l, u, vt, tt):
    n, d = x.shape
    tt = min(tt, n)
    nec = PEER_EXPERTS // PEER_ECHUNK
    const = lambda shape: pl.BlockSpec(shape, lambda i, e: (0,) * len(shape))
    key_spec = const((PEER_NKEYS, PEER_DQH))
    stat = pltpu.VMEM((PEER_HEADS, PEER_NKEYS, tt), F32)
    return pl.pallas_call(
        _peer_kernel,
        grid=(n // tt, nec),
        in_specs=[pl.BlockSpec((tt, d), lambda i, e: (i, 0)),
                  const((1, d)),
                  const((PEER_HEADS * PEER_DQ, d)),
                  key_spec, key_spec, key_spec, key_spec,
                  pl.BlockSpec((PEER_ECHUNK, d), lambda i, e: (e, 0)),
                  pl.BlockSpec((d, PEER_ECHUNK), lambda i, e: (0, e))],
        out_specs=pl.BlockSpec((tt, d), lambda i, e: (i, 0)),
        out_shape=jax.ShapeDtypeStruct((n, d), F32),
        scratch_shapes=[pltpu.VMEM((d, tt), BF16), stat, stat, stat, stat,
                        pltpu.VMEM((PEER_ECHUNK, tt), BF16),
                        pltpu.VMEM((d, tt), F32)],
        compiler_params=_params("parallel", "arbitrary"),
    )(x, g_ffn.reshape(1, d), wqt, k1h, k1l, k2h, k2l, u, vt)


def _split_hi_lo(a):
    hi = a.astype(BF16)
    return hi, (a - hi.astype(F32)).astype(BF16)


def _prepare_weights(g_mix, w_in, gla_w_dec_f, gla_b_dec_f, gla_w_dec_b, gla_b_dec_b, gla_g_out, gla_w_o,
                     diff_g_q, diff_g_k, diff_lq1, diff_lk1, diff_lq2, diff_lk2, diff_g_sub, diff_w_o,
                     mem_g_norm, mem_w_kv, mem_g_q, mem_g_k, mem_w_o, w_out,
                     g_ffn, peer_w_q, peer_sub_k1, peer_sub_k2, peer_u, peer_v):
    w = w_in[0]
    lr0 = 2 * GLA_QK + 2 * GLA_V
    lr1 = lr0 + 2 * GLA_LOWRANK
    w_main = jnp.concatenate([w[:, :lr0], w[:, lr1:]], axis=1).astype(BF16)
    w_lr = jnp.pad(w[:, lr0:lr1], ((0, 0), (0, LANES - 2 * GLA_LOWRANK))).astype(BF16)
    k1h, k1l = _split_hi_lo(peer_sub_k1[0])
    k2h, k2l = _split_hi_lo(peer_sub_k2[0])
    return dict(
        g_mix=g_mix[0], w_main=w_main, w_lr=w_lr,
        gla=(gla_w_dec_f[0], gla_b_dec_f[0], gla_w_dec_b[0], gla_b_dec_b[0], gla_g_out[0]),
        diff=(diff_g_q[0], diff_g_k[0], diff_lq1[0], diff_lk1[0], diff_lq2[0], diff_lk2[0], diff_g_sub[0]),
        mem_g_norm=mem_g_norm[0], mem_w_kv=mem_w_kv[0].astype(BF16), mem_gq=mem_g_q[0], mem_gk=mem_g_k[0],
        w_o=(gla_w_o[0].astype(BF16), diff_w_o[0].astype(BF16), mem_w_o[0].astype(BF16), w_out[0].astype(BF16)),
        g_ffn=g_ffn[0], wqt=peer_w_q[0].T.astype(BF16), keys=(k1h, k1l, k2h, k2l),
        u=peer_u[0].astype(BF16), vt=peer_v[0].T.astype(BF16),
    )


def _trunk(x, mem, p):
    bsz, t, d = x.shape
    n = bsz * t
    lam_init = 0.8 - 0.6 * math.exp(-0.3 * 0)
    x2 = x.reshape(n, d)
    proj = _rms_matmul(x2, p["g_mix"], p["w_main"], BF16, 1024, 2048).reshape(bsz, t, MAIN_COLS)
    lr = _rms_matmul(x2, p["g_mix"], p["w_lr"], F32, 1024, LANES).reshape(bsz, t, LANES)
    kv = _rms_matmul(mem.reshape(bsz * N_MEM, d), p["mem_g_norm"], p["mem_w_kv"], F32, 512, 2 * MEM_Q)
    kv = kv.reshape(bsz, N_MEM, 2 * MEM_Q)
    pre_gla = _gla_branch(proj, lr, *p["gla"], tb=256)
    pre_diff = _diff_branch(proj, *p["diff"], lam_init=lam_init, tm=512, tq=512, tk=512)
    pre_mem = _mem_branch(proj, kv, p["mem_gq"], p["mem_gk"], tm=512)
    x1 = _merge(x2, proj.reshape(n, MAIN_COLS), pre_gla.reshape(n, d), pre_diff.reshape(n, d),
                pre_mem.reshape(n, d), *p["w_o"], tm=512)
    y = _peer(x1, p["g_ffn"], p["wqt"], *p["keys"], p["u"], p["vt"], tt=512)
    return y.reshape(bsz, t, d)


def kernel(x_prompt, x_sample, mem_prompt, mem_sample, g_mix, w_in, gla_w_dec_f, gla_b_dec_f, gla_w_dec_b, gla_b_dec_b, gla_g_out, gla_w_o, diff_g_q, diff_g_k, diff_lq1, diff_lk1, diff_lq2, diff_lk2, diff_g_sub, diff_w_o, mem_g_norm, mem_w_kv, mem_g_q, mem_g_k, mem_w_o, w_out, g_ffn, peer_w_q, peer_sub_k1, peer_sub_k2, peer_u, peer_v):
    p = _prepare_weights(g_mix, w_in, gla_w_dec_f, gla_b_dec_f, gla_w_dec_b, gla_b_dec_b, gla_g_out, gla_w_o,
                         diff_g_q, diff_g_k, diff_lq1, diff_lk1, diff_lq2, diff_lk2, diff_g_sub, diff_w_o,
                         mem_g_norm, mem_w_kv, mem_g_q, mem_g_k, mem_w_o, w_out,
                         g_ffn, peer_w_q, peer_sub_k1, peer_sub_k2, peer_u, peer_v)
    return (_trunk(x_prompt, mem_prompt, p), _trunk(x_sample, mem_sample, p))
```

```python
import functools
import math

import jax
import jax.numpy as jnp
from jax import lax
from jax.experimental import pallas as pl
from jax.experimental.pallas import tpu as pltpu

F32 = jnp.float32
BF16 = jnp.bfloat16

D_MODEL = 1024
N_MEM = 256
RMS_EPS = 1e-6
ROPE_THETA = 10000.0

GLA_HEADS = 4
GLA_DK = 128
GLA_DV = 256
GLA_LOWRANK = 16
GLA_GATE_NORM = 16.0
GLA_QK = GLA_HEADS * GLA_DK
GLA_V = GLA_HEADS * GLA_DV

DIFF_HEADS = 8
DIFF_DH = 64
DIFF_DV = 2 * DIFF_DH
DIFF_QK = DIFF_HEADS * 2 * DIFF_DH
DIFF_V = DIFF_HEADS * DIFF_DV

MEM_HEADS = 4
MEM_DH = 256
MEM_Q = MEM_HEADS * MEM_DH

N_BRANCH = 3
PEER_HEADS = 8
PEER_NKEYS = 128
PEER_EXPERTS = PEER_NKEYS * PEER_NKEYS
PEER_DQ = 256
PEER_DQH = PEER_DQ // 2
PEER_TOPK = 16

LANES = 128
SUBLANES = 8
VMEM_LIMIT = 56 * 1024 * 1024

OFF_GQ = 0
OFF_GK = OFF_GQ + GLA_QK
OFF_GV = OFF_GK + GLA_QK
OFF_GG = OFF_GV + GLA_V
OFF_DQ = OFF_GG + GLA_V
OFF_DK = OFF_DQ + DIFF_QK
OFF_DV = OFF_DK + DIFF_QK
OFF_MQ = OFF_DV + DIFF_V
OFF_BR = OFF_MQ + MEM_Q
MAIN_COLS = OFF_BR + N_BRANCH * D_MODEL

LOG2E = 1.4426950408889634


def _dot(a, b):
    return jnp.dot(a, b, preferred_element_type=F32)


def _dot_nt(a, b):
    return lax.dot_general(a, b, (((1,), (1,)), ((), ())), preferred_element_type=F32)


def _params(*sem):
    return pltpu.CompilerParams(dimension_semantics=sem, vmem_limit_bytes=VMEM_LIMIT)


def _rms_matmul_kernel(x_ref, g_ref, w_ref, o_ref, h_ref):
    @pl.when(pl.program_id(1) == 0)
    def _():
        x = x_ref[...]
        ms = jnp.mean(x * x, axis=-1, keepdims=True)
        h_ref[...] = (x * lax.rsqrt(ms + RMS_EPS) * g_ref[...]).astype(BF16)

    o_ref[...] = _dot(h_ref[...], w_ref[...]).astype(o_ref.dtype)


def _rms_matmul(x, g, w, out_dtype, tm, tn):
    n, d = x.shape
    m = w.shape[1]
    tm = min(tm, n)
    tn = min(tn, m)
    return pl.pallas_call(
        _rms_matmul_kernel,
        grid=(n // tm, m // tn),
        in_specs=[pl.BlockSpec((tm, d), lambda i, j: (i, 0)),
                  pl.BlockSpec((1, d), lambda i, j: (0, 0)),
                  pl.BlockSpec((d, tn), lambda i, j: (0, j))],
        out_specs=pl.BlockSpec((tm, tn), lambda i, j: (i, j)),
        out_shape=jax.ShapeDtypeStruct((n, m), out_dtype),
        scratch_shapes=[pltpu.VMEM((tm, d), BF16)],
        compiler_params=_params("parallel", "arbitrary"),
    )(x, g.reshape(1, d), w)


GLA_CHUNK = 128


def _gla_kernel(*refs, reverse, nchunk, final):
    if final:
        (q_ref, k_ref, v_ref, lr_ref, wd_ref, bd_ref, of_ref, gate_ref, gout_ref, o_ref, st_ref) = refs
    else:
        (q_ref, k_ref, v_ref, lr_ref, wd_ref, bd_ref, o_ref, st_ref) = refs
    C = GLA_CHUNK

    @pl.when(pl.program_id(1) == 0)
    def _():
        st_ref[...] = jnp.zeros_like(st_ref)

    rows = lax.broadcasted_iota(jnp.int32, (C, C), 0)
    cols = lax.broadcasted_iota(jnp.int32, (C, C), 1)
    tri = jnp.where(cols <= rows, 1.0, 0.0).astype(BF16)
    keep = (cols >= rows) if reverse else (cols <= rows)
    scale = GLA_DK ** -0.5
    order = range(nchunk - 1, -1, -1) if reverse else range(nchunk)
    for ci in order:
        sl = pl.ds(ci * C, C)
        lr_c = lr_ref[0, sl, :].astype(BF16)
        for h in range(GLA_HEADS):
            dk = slice(h * GLA_DK, (h + 1) * GLA_DK)
            dv = slice(h * GLA_DV, (h + 1) * GLA_DV)
            q = q_ref[0, sl, dk].astype(F32) * scale
            k = k_ref[0, sl, dk].astype(F32)
            v = v_ref[0, sl, dv]
            pre = _dot(lr_c, wd_ref[h]) + bd_ref[h]
            la = (jnp.minimum(pre, 0.0) - jnp.log1p(jnp.exp(-jnp.abs(pre)))) * (1.0 / GLA_GATE_NORM)
            la_hi = la.astype(BF16)
            la_lo = (la - la_hi.astype(F32)).astype(BF16)
            b = _dot(tri, la_hi) + _dot(tri, la_lo)
            tot = b[C - 1:C, :]
            st = st_ref[h]
            vt = v.astype(F32).T.astype(BF16)
            if not reverse:
                q_in = (q * jnp.exp(b)).astype(BF16)
                q_st = q_in
                k_in = (k * jnp.exp(-b)).astype(BF16)
                k_st = (k * jnp.exp(tot - b)).astype(BF16)
            else:
                c = b - la
                q_in = (q * jnp.exp(-c)).astype(BF16)
                q_st = (q * jnp.exp(tot - c)).astype(BF16)
                k_in = (k * jnp.exp(c)).astype(BF16)
                k_st = k_in
            att = jnp.where(keep, _dot_nt(q_in, k_in), 0.0)
            o = _dot(att.astype(BF16), v) + _dot_nt(q_st, st.astype(BF16))
            st_ref[h] = jnp.exp(tot) * st + _dot(vt, k_st)
            if final:
                diag = jnp.sum(q * k, axis=-1, keepdims=True)
                o = of_ref[0, sl, dv] + o - diag * v.astype(F32)
                ms = jnp.mean(o * o, axis=-1, keepdims=True)
                o = o * lax.rsqrt(ms + RMS_EPS) * gout_ref[...]
                gt = gate_ref[0, sl, dv].astype(F32)
                o = o * (gt * jax.nn.sigmoid(gt))
            o_ref[0, sl, dv] = o.astype(o_ref.dtype)


def _gla_branch(proj, lr, w_dec_f, b_dec_f, w_dec_b, b_dec_b, g_out, tb):
    bsz, t, _ = proj.shape
    tb = min(tb, t)
    nblk = t // tb
    nchunk = tb // GLA_CHUNK

    def dec_weights(w_dec, b_dec, row0):
        w = jnp.zeros((GLA_HEADS, LANES, GLA_DK), F32)
        w = w.at[:, row0:row0 + GLA_LOWRANK, :].set(
            w_dec.reshape(GLA_LOWRANK, GLA_HEADS, GLA_DK).transpose(1, 0, 2))
        return w.astype(BF16), b_dec.reshape(GLA_HEADS, 1, GLA_DK).astype(F32)

    def call(reverse, o_fwd):
        wd, bd = dec_weights(w_dec_b, b_dec_b, GLA_LOWRANK) if reverse else dec_weights(w_dec_f, b_dec_f, 0)
        blk = (lambda j: nblk - 1 - j) if reverse else (lambda j: j)
        in_specs = [
            pl.BlockSpec((1, tb, GLA_QK), lambda b, j: (b, blk(j), OFF_GQ // GLA_QK)),
            pl.BlockSpec((1, tb, GLA_QK), lambda b, j: (b, blk(j), OFF_GK // GLA_QK)),
            pl.BlockSpec((1, tb, GLA_V), lambda b, j: (b, blk(j), OFF_GV // GLA_V)),
            pl.BlockSpec((1, tb, LANES), lambda b, j: (b, blk(j), 0)),
            pl.BlockSpec((GLA_HEADS, LANES, GLA_DK), lambda b, j: (0, 0, 0)),
            pl.BlockSpec((GLA_HEADS, 1, GLA_DK), lambda b, j: (0, 0, 0)),
        ]
        args = [proj, proj, proj, lr, wd, bd]
        if reverse:
            in_specs += [
                pl.BlockSpec((1, tb, GLA_V), lambda b, j: (b, blk(j), 0)),
                pl.BlockSpec((1, tb, GLA_V), lambda b, j: (b, blk(j), OFF_GG // GLA_V)),
                pl.BlockSpec((1, GLA_DV), lambda b, j: (0, 0)),
            ]
            args += [o_fwd, proj, g_out.reshape(1, GLA_DV).astype(F32)]
        return pl.pallas_call(
            functools.partial(_gla_kernel, reverse=reverse, nchunk=nchunk, final=reverse),
            grid=(bsz, nblk),
            in_specs=in_specs,
            out_specs=pl.BlockSpec((1, tb, GLA_V), lambda b, j: (b, blk(j), 0)),
            out_shape=jax.ShapeDtypeStruct((bsz, t, GLA_V), BF16 if reverse else F32),
            scratch_shapes=[pltpu.VMEM((GLA_HEADS, GLA_DV, GLA_DK), F32)],
            compiler_params=_params("parallel", "arbitrary"),
        )(*args)

    return call(True, call(False, None))


def _group_sumsq(x, ones_bd):
    x2 = x * x
    hi = x2.astype(BF16)
    lo = (x2 - hi.astype(F32)).astype(BF16)
    pieces = []
    for c in range(x.shape[1] // LANES):
        cs = slice(c * LANES, (c + 1) * LANES)
        pieces.append(_dot(hi[:, cs], ones_bd) + _dot(lo[:, cs], ones_bd))
    return jnp.concatenate(pieces, axis=1)


def _diff_prep_kernel(q_ref, k_ref, gq_ref, gk_ref, cos_ref, sin_ref, qz_ref, kt_ref):
    tm = q_ref.shape[1]
    r = lax.broadcasted_iota(jnp.int32, (LANES, LANES), 0) // DIFF_DH
    c = lax.broadcasted_iota(jnp.int32, (LANES, LANES), 1) // DIFF_DH
    ones_bd = jnp.where(r == c, 1.0, 0.0).astype(BF16)
    lane = lax.broadcasted_iota(jnp.int32, (tm, DIFF_QK), 1)
    first_half = (lane % DIFF_DH) < (DIFF_DH // 2)
    reps = DIFF_QK // LANES
    cos = jnp.concatenate([cos_ref[...]] * reps, axis=1)
    sin = jnp.concatenate([sin_ref[...]] * reps, axis=1)

    def norm_rope(x, g):
        ms = _group_sumsq(x, ones_bd) * (1.0 / DIFF_DH)
        xn = x * lax.rsqrt(ms + RMS_EPS) * g
        partner = jnp.where(first_half,
                            pltpu.roll(xn, DIFF_QK - DIFF_DH // 2, axis=1),
                            pltpu.roll(xn, DIFF_DH // 2, axis=1))
        return xn * cos + partner * sin

    qr = norm_rope(q_ref[0].astype(F32), gq_ref[...]) * (DIFF_DH ** -0.5 * LOG2E)
    kr = norm_rope(k_ref[0].astype(F32), gk_ref[...])
    lane_h = lax.broadcasted_iota(jnp.int32, (tm, LANES), 1)
    pieces = []
    for h in range(DIFF_HEADS):
        qh = qr[:, h * LANES:(h + 1) * LANES]
        pieces.append(jnp.where(lane_h < DIFF_DH, qh, 0.0))
        pieces.append(jnp.where(lane_h >= DIFF_DH, qh, 0.0))
        kt_ref[0, h] = kr[:, h * LANES:(h + 1) * LANES].T.astype(BF16)
    qz_ref[0] = jnp.concatenate(pieces, axis=1).astype(BF16)


FLASH_SAFE_LOG2_RANGE = 60.0


def _flash_kernel(bound_ref, qz_ref, kt_ref, v_ref, lq1_ref, lk1_ref, lq2_ref, lk2_ref, gsub_ref, o_ref,
                  qs_ref, m_ref, l_ref, acc_ref, *, tq, tk, nk, lam_init):
    qs_ref[0:tq, :] = qz_ref[0, :, 0:LANES]
    qs_ref[tq:2 * tq, :] = qz_ref[0, :, LANES:2 * LANES]
    reps = tk // LANES
    bound = bound_ref[0]

    def scores(i):
        k0 = pl.multiple_of(i * tk, tk)
        s = _dot(qs_ref[...], kt_ref[0, 0, :, pl.ds(k0, tk)])
        return s, v_ref[0, pl.ds(k0, tk), :]

    def finalize(o):
        lam = (jnp.exp(jnp.sum(lq1_ref[...] * lk1_ref[...], axis=1, keepdims=True))
               - jnp.exp(jnp.sum(lq2_ref[...] * lk2_ref[...], axis=1, keepdims=True)) + lam_init)
        o = o[0:tq] - lam * o[tq:2 * tq]
        ms = jnp.mean(o * o, axis=-1, keepdims=True)
        o = o * lax.rsqrt(ms + RMS_EPS) * gsub_ref[...] * (1.0 - lam_init)
        o_ref[0] = o.astype(o_ref.dtype)

    @pl.when(bound <= FLASH_SAFE_LOG2_RANGE)
    def _():
        l_ref[...] = jnp.zeros_like(l_ref)
        acc_ref[...] = jnp.zeros_like(acc_ref)

        def step(i, carry):
            s, v = scores(i)
            p = jnp.exp2(s - bound)
            part = p[:, 0:LANES]
            for r in range(1, reps):
                part = part + p[:, r * LANES:(r + 1) * LANES]
            l_ref[...] += part
            acc_ref[...] += _dot(p.astype(BF16), v)
            return carry

        lax.fori_loop(0, nk, step, 0)
        finalize(acc_ref[...] / jnp.sum(l_ref[...], axis=1, keepdims=True))

    @pl.when(bound > FLASH_SAFE_LOG2_RANGE)
    def _():
        m_ref[...] = jnp.full_like(m_ref, -jnp.inf)
        l_ref[...] = jnp.zeros_like(l_ref)
        acc_ref[...] = jnp.zeros_like(acc_ref)

        def step(i, carry):
            s, v = scores(i)
            m_prev = m_ref[...]
            m_new = jnp.maximum(m_prev, jnp.max(s, axis=1, keepdims=True))
            alpha = jnp.exp2(m_prev - m_new)
            p = jnp.exp2(s - jnp.concatenate([m_new] * reps, axis=1))
            l_ref[...] = alpha * l_ref[...] + jnp.sum(p, axis=1, keepdims=True)
            acc_ref[...] = alpha * acc_ref[...] + _dot(p.astype(BF16), v)
            m_ref[...] = m_new
            return carry

        lax.fori_loop(0, nk, step, 0)
        finalize(acc_ref[...] / l_ref[...])


def _rope_tables(t):
    half = DIFF_DH // 2
    inv_freq = ROPE_THETA ** (-jnp.arange(0, DIFF_DH, 2, dtype=F32) / DIFF_DH)
    ang = jnp.arange(t, dtype=F32)[:, None] * inv_freq[None, :]
    cos, sin = jnp.cos(ang), jnp.sin(ang)
    cos_t = jnp.tile(cos, (1, LANES // half))
    sin_t = jnp.tile(jnp.concatenate([-sin, sin], axis=1), (1, LANES // DIFF_DH))
    return cos_t, sin_t


def _diff_branch(proj, g_q, g_k, lq1, lk1, lq2, lk2, g_sub, lam_init, tm, tq, tk):
    bsz, t, _ = proj.shape
    tm, tq, tk = min(tm, t), min(tq, t), min(tk, t)
    cos_t, sin_t = _rope_tables(t)
    tile_g = lambda g: jnp.tile(g.astype(F32), DIFF_QK // DIFF_DH).reshape(1, DIFF_QK)
    qz, kt = pl.pallas_call(
        _diff_prep_kernel,
        grid=(bsz, t // tm),
        in_specs=[pl.BlockSpec((1, tm, DIFF_QK), lambda b, i: (b, i, OFF_DQ // DIFF_QK)),
                  pl.BlockSpec((1, tm, DIFF_QK), lambda b, i: (b, i, OFF_DK // DIFF_QK)),
                  pl.BlockSpec((1, DIFF_QK), lambda b, i: (0, 0)),
                  pl.BlockSpec((1, DIFF_QK), lambda b, i: (0, 0)),
                  pl.BlockSpec((tm, LANES), lambda b, i: (i, 0)),
                  pl.BlockSpec((tm, LANES), lambda b, i: (i, 0))],
        out_specs=[pl.BlockSpec((1, tm, 2 * DIFF_QK), lambda b, i: (b, i, 0)),
                   pl.BlockSpec((1, DIFF_HEADS, LANES, tm), lambda b, i: (b, 0, 0, i))],
        out_shape=[jax.ShapeDtypeStruct((bsz, t, 2 * DIFF_QK), BF16),
                   jax.ShapeDtypeStruct((bsz, DIFF_HEADS, LANES, t), BF16)],
        compiler_params=_params("parallel", "parallel"),
    )(proj, proj, tile_g(g_q), tile_g(g_k), cos_t, sin_t)

    vec = lambda a: a.reshape(1, DIFF_DH).astype(F32)
    bound = (1.01 * DIFF_DH * (DIFF_DH ** -0.5 * LOG2E)
             * jnp.max(jnp.abs(g_q)) * jnp.max(jnp.abs(g_k))).astype(F32).reshape(1)
    return pl.pallas_call(
        functools.partial(_flash_kernel, tq=tq, tk=tk, nk=t // tk, lam_init=lam_init),
        grid=(bsz, DIFF_HEADS, t // tq),
        in_specs=[pl.BlockSpec(memory_space=pltpu.SMEM),
                  pl.BlockSpec((1, tq, 2 * LANES), lambda b, h, i: (b, i, h)),
                  pl.BlockSpec((1, 1, LANES, t), lambda b, h, i: (b, h, 0, 0)),
                  pl.BlockSpec((1, t, DIFF_DV), lambda b, h, i: (b, 0, OFF_DV // DIFF_DV + h)),
                  pl.BlockSpec((1, DIFF_DH), lambda b, h, i: (0, 0)),
                  pl.BlockSpec((1, DIFF_DH), lambda b, h, i: (0, 0)),
                  pl.BlockSpec((1, DIFF_DH), lambda b, h, i: (0, 0)),
                  pl.BlockSpec((1, DIFF_DH), lambda b, h, i: (0, 0)),
                  pl.BlockSpec((1, DIFF_DV), lambda b, h, i: (0, 0))],
        out_specs=pl.BlockSpec((1, tq, DIFF_DV), lambda b, h, i: (b, i, h)),
        out_shape=jax.ShapeDtypeStruct((bsz, t, DIFF_V), BF16),
        scratch_shapes=[pltpu.VMEM((2 * tq, LANES), BF16),
                        pltpu.VMEM((2 * tq, LANES), F32),
                        pltpu.VMEM((2 * tq, LANES), F32),
                        pltpu.VMEM((2 * tq, DIFF_DV), F32)],
        compiler_params=_params("parallel", "parallel", "arbitrary"),
    )(bound, qz, kt, proj, vec(lq1), vec(lk1), vec(lq2), vec(lk2), g_sub.reshape(1, DIFF_DV).astype(F32))


def _mem_kernel(q_ref, kv_ref, gq_ref, gk_ref, o_ref):
    outs = []
    for h in range(MEM_HEADS):
        hs = slice(h * MEM_DH, (h + 1) * MEM_DH)
        q = q_ref[0, :, hs].astype(F32)
        q = q * lax.rsqrt(jnp.mean(q * q, axis=-1, keepdims=True) + RMS_EPS) * gq_ref[...]
        k = kv_ref[0, :, hs]
        k = k * lax.rsqrt(jnp.mean(k * k, axis=-1, keepdims=True) + RMS_EPS) * gk_ref[...]
        v = kv_ref[0, :, MEM_Q + h * MEM_DH:MEM_Q + (h + 1) * MEM_DH]
        s = _dot_nt(q.astype(BF16), k.astype(BF16)) * (MEM_DH ** -0.5)
        p = jnp.exp(s - jnp.max(s, axis=-1, keepdims=True))
        p = p / jnp.sum(p, axis=-1, keepdims=True)
        outs.append(_dot(p.astype(BF16), v.astype(BF16)))
    o_ref[0] = jnp.concatenate(outs, axis=1).astype(o_ref.dtype)


def _mem_branch(proj, kv, g_q, g_k, tm):
    bsz, t, _ = proj.shape
    tm = min(tm, t)
    return pl.pallas_call(
        _mem_kernel,
        grid=(bsz, t // tm),
        in_specs=[pl.BlockSpec((1, tm, MEM_Q), lambda b, i: (b, i, OFF_MQ // MEM_Q)),
                  pl.BlockSpec((1, N_MEM, 2 * MEM_Q), lambda b, i: (b, 0, 0)),
                  pl.BlockSpec((1, MEM_DH), lambda b, i: (0, 0)),
                  pl.BlockSpec((1, MEM_DH), lambda b, i: (0, 0))],
        out_specs=pl.BlockSpec((1, tm, MEM_Q), lambda b, i: (b, i, 0)),
        out_shape=jax.ShapeDtypeStruct((bsz, t, MEM_Q), BF16),
        compiler_params=_params("parallel", "parallel"),
    )(proj, kv, g_q.reshape(1, MEM_DH).astype(F32), g_k.reshape(1, MEM_DH).astype(F32))


def _merge_kernel(x_ref, pg_ref, pd_ref, pm_ref, g0_ref, g1_ref, g2_ref,
                  wg_ref, wd_ref, wm_ref, wo_ref, o_ref):
    merged = (jax.nn.sigmoid(g0_ref[...].astype(F32)) * _dot(pg_ref[...], wg_ref[...])
              + jax.nn.sigmoid(g1_ref[...].astype(F32)) * _dot(pd_ref[...], wd_ref[...])
              + jax.nn.sigmoid(g2_ref[...].astype(F32)) * _dot(pm_ref[...], wm_ref[...]))
    o_ref[...] = x_ref[...] + _dot(merged.astype(BF16), wo_ref[...])


def _merge(x, proj, pg, pd, pm, wg, wd, wm, wo, tm):
    n, d = x.shape
    tm = min(tm, n)
    row = lambda c: pl.BlockSpec((tm, d), lambda i: (i, c))
    full = pl.BlockSpec((d, d), lambda i: (0, 0))
    br = OFF_BR // d
    return pl.pallas_call(
        _merge_kernel,
        grid=(n // tm,),
        in_specs=[row(0), row(0), row(0), row(0), row(br), row(br + 1), row(br + 2), full, full, full, full],
        out_specs=row(0),
        out_shape=jax.ShapeDtypeStruct((n, d), F32),
        compiler_params=_params("parallel"),
    )(x, pg, pd, pm, proj, proj, proj, wg, wd, wm, wo)


PEER_ECHUNK = 1024
PEER_IPER = PEER_ECHUNK // PEER_NKEYS
PEER_IGROUP = 8
PEER_JBLK = 16


def _bitonic_merge_desc(a):
    n = len(a)
    j = n // 2
    while j >= 1:
        for i in range(n):
            l = i ^ j
            if l > i:
                a[i], a[l] = jnp.maximum(a[i], a[l]), jnp.minimum(a[i], a[l])
        j //= 2
    return a


def _top16_desc(vals):
    a = list(vals)
    n = len(a)
    k = 2
    while k <= n:
        j = k // 2
        while j >= 1:
            for i in range(n):
                l = i ^ j
                if l > i:
                    hi, lo = jnp.maximum(a[i], a[l]), jnp.minimum(a[i], a[l])
                    a[i], a[l] = (hi, lo) if (i & k) == 0 else (lo, hi)
            j //= 2
        k *= 2
    for shift in (4, 2, 1):
        a = [jnp.maximum(a[r], pltpu.roll(a[n - 1 - r], shift, axis=0)) for r in range(n)]
        a = _bitonic_merge_desc(a)
    return a


def _peer_kernel(x_ref, g_ref, wqt_ref, k1h_ref, k1l_ref, k2h_ref, k2l_ref, u_ref, vt_ref, o_ref,
                 hbt_ref, t_ref, c_ref, e2_ref, a_ref, wt_ref, acc_ref):
    ec = pl.program_id(1)
    tt = x_ref.shape[0]
    neg = jnp.full((SUBLANES, tt), -jnp.inf, F32)
    sub = lax.broadcasted_iota(jnp.int32, (SUBLANES, tt), 0)

    @pl.when(ec == 0)
    def _():
        x = x_ref[...]
        hb = x * lax.rsqrt(jnp.mean(x * x, axis=-1, keepdims=True) + RMS_EPS) * g_ref[...]
        hbt_ref[...] = hb.T.astype(BF16)
        acc_ref[...] = jnp.zeros_like(acc_ref)
        for h in range(PEER_HEADS):
            def scores(kh_ref, kl_ref, row0):
                qt = _dot(wqt_ref[row0:row0 + PEER_DQH, :], hbt_ref[...])
                qh = qt.astype(BF16)
                ql = (qt - qh.astype(F32)).astype(BF16)
                return _dot(kh_ref[...], qh) + _dot(kh_ref[...], ql) + _dot(kl_ref[...], qh)

            s1 = scores(k1h_ref, k1l_ref, h * PEER_DQ)
            s2 = scores(k2h_ref, k2l_ref, h * PEER_DQ + PEER_DQH)
            blocks = lambda s: [s[r * SUBLANES:(r + 1) * SUBLANES, :] for r in range(PEER_NKEYS // SUBLANES)]
            v1 = _top16_desc(blocks(s1))
            v2 = _top16_desc(blocks(s2))

            def pack(vs):
                out = neg
                for r in range(SUBLANES):
                    out = jnp.where(sub == r, vs[r], out)
                return out

            v2_lo, v2_hi, v1_hi = pack(v2[:SUBLANES]), pack(v2[SUBLANES:]), pack(v1[SUBLANES:])
            cand = [v1[0] + v2_lo, v1[0] + v2_hi, v1_hi + v2[0]]
            cand += [v1[a] + v2_lo for a in range(1, SUBLANES)]
            cand += [neg] * (PEER_TOPK - len(cand))
            top = _top16_desc(cand)
            tau16 = top[PEER_TOPK - 1]
            below = lambda s, bound: jnp.max(jnp.where(s < bound, s, -jnp.inf), axis=0, keepdims=True)
            tau17 = below(jnp.concatenate(cand[:10], axis=0), tau16[0:1, :])
            tau17 = jnp.maximum(tau17, below(s1, v1[PEER_TOPK - 1][0:1, :]) + v2[0][0:1, :])
            tau17 = jnp.maximum(tau17, below(s2, v2[PEER_TOPK - 1][0:1, :]) + v1[0][0:1, :])
            tau = 0.5 * (tau16[0:1, :] + tau17)
            z = top[0] - top[0]
            for r in range(PEER_TOPK):
                z = z + jnp.exp(top[r] - top[0])
            z = z[0:1, :]
            nchunks = PEER_NKEYS // PEER_IPER
            m2 = v2[0][0:1, :]
            t_ref[h] = jnp.exp(tau - s1 - m2).reshape(nchunks, PEER_IPER, tt)
            c_ref[h] = (jnp.exp(s1 - v1[0][0:1, :]) / z).reshape(nchunks, PEER_IPER, tt)
            e2_ref[h] = jnp.exp(s2 - m2)

    a_ref[...] = _dot(u_ref[...], hbt_ref[...])
    for lb in range(tt // LANES):
        ls = slice(lb * LANES, (lb + 1) * LANES)
        for jb in range(PEER_NKEYS // PEER_JBLK):
            js = slice(jb * PEER_JBLK, (jb + 1) * PEER_JBLK)
            for ig in range(PEER_IPER // PEER_IGROUP):
                accs = [None] * PEER_IGROUP
                for h in range(PEER_HEADS):
                    e2v = e2_ref[h, js, ls]
                    for q in range(PEER_IGROUP):
                        il = ig * PEER_IGROUP + q
                        thr = t_ref[h, ec, il:il + 1, ls]
                        coef = c_ref[h, ec, il:il + 1, ls]
                        term = jnp.where(e2v >= thr, e2v, 0.0) * coef
                        accs[q] = term if accs[q] is None else accs[q] + term
                for q in range(PEER_IGROUP):
                    il = ig * PEER_IGROUP + q
                    rs = slice(il * PEER_NKEYS + jb * PEER_JBLK, il * PEER_NKEYS + (jb + 1) * PEER_JBLK)
                    a = a_ref[rs, ls]
                    gelu = 0.5 * a * (1.0 + lax.erf(a * (2.0 ** -0.5)))
                    wt_ref[rs, ls] = (gelu * accs[q]).astype(BF16)
    acc_ref[...] += _dot(vt_ref[...], wt_ref[...])

    @pl.when(ec == pl.num_programs(1) - 1)
    def _():
        o_ref[...] = x_ref[...] + acc_ref[...].T


def _peer(x, g_ffn, wqt, k1h, k1l, k2h, k2l, u, vt, tt):
    n, d = x.shape
    tt = min(tt, n)
    nec = PEER_EXPERTS // PEER_ECHUNK
    const = lambda shape: pl.BlockSpec(shape, lambda i, e: (0,) * len(shape))
    key_spec = const((PEER_NKEYS, PEER_DQH))
    stat = pltpu.VMEM((PEER_HEADS, PEER_NKEYS, tt), F32)
    stat_i = pltpu.VMEM((PEER_HEADS, PEER_NKEYS // PEER_IPER, PEER_IPER, tt), F32)
    return pl.pallas_call(
        _peer_kernel,
        grid=(n // tt, nec),
        in_specs=[pl.BlockSpec((tt, d), lambda i, e: (i, 0)),
                  const((1, d)),
                  const((PEER_HEADS * PEER_DQ, d)),
                  key_spec, key_spec, key_spec, key_spec,
                  pl.BlockSpec((PEER_ECHUNK, d), lambda i, e: (e, 0)),
                  pl.BlockSpec((d, PEER_ECHUNK), lambda i, e: (0, e))],
        out_specs=pl.BlockSpec((tt, d), lambda i, e: (i, 0)),
        out_shape=jax.ShapeDtypeStruct((n, d), F32),
        scratch_shapes=[pltpu.VMEM((d, tt), BF16), stat_i, stat_i, stat,
                        pltpu.VMEM((PEER_ECHUNK, tt), F32),
                        pltpu.VMEM((PEER_ECHUNK, tt), BF16),
                        pltpu.VMEM((d, tt), F32)],
        compiler_params=_params("parallel", "arbitrary"),
    )(x, g_ffn.reshape(1, d), wqt, k1h, k1l, k2h, k2l, u, vt)


def _split_hi_lo(a):
    hi = a.astype(BF16)
    return hi, (a - hi.astype(F32)).astype(BF16)


def _prepare_weights(g_mix, w_in, gla_w_dec_f, gla_b_dec_f, gla_w_dec_b, gla_b_dec_b, gla_g_out, gla_w_o,
                     diff_g_q, diff_g_k, diff_lq1, diff_lk1, diff_lq2, diff_lk2, diff_g_sub, diff_w_o,
                     mem_g_norm, mem_w_kv, mem_g_q, mem_g_k, mem_w_o, w_out,
                     g_ffn, peer_w_q, peer_sub_k1, peer_sub_k2, peer_u, peer_v):
    w = w_in[0]
    lr0 = 2 * GLA_QK + 2 * GLA_V
    lr1 = lr0 + 2 * GLA_LOWRANK
    w_main = jnp.concatenate([w[:, :lr0], w[:, lr1:]], axis=1).astype(BF16)
    w_lr = jnp.pad(w[:, lr0:lr1], ((0, 0), (0, LANES - 2 * GLA_LOWRANK))).astype(BF16)
    k1h, k1l = _split_hi_lo(peer_sub_k1[0])
    k2h, k2l = _split_hi_lo(peer_sub_k2[0])
    return dict(
        g_mix=g_mix[0], w_main=w_main, w_lr=w_lr,
        gla=(gla_w_dec_f[0], gla_b_dec_f[0], gla_w_dec_b[0], gla_b_dec_b[0], gla_g_out[0]),
        diff=(diff_g_q[0], diff_g_k[0], diff_lq1[0], diff_lk1[0], diff_lq2[0], diff_lk2[0], diff_g_sub[0]),
        mem_g_norm=mem_g_norm[0], mem_w_kv=mem_w_kv[0].astype(BF16), mem_gq=mem_g_q[0], mem_gk=mem_g_k[0],
        w_o=(gla_w_o[0].astype(BF16), diff_w_o[0].astype(BF16), mem_w_o[0].astype(BF16), w_out[0].astype(BF16)),
        g_ffn=g_ffn[0], wqt=peer_w_q[0].T.astype(BF16), keys=(k1h, k1l, k2h, k2l),
        u=peer_u[0].astype(BF16), vt=peer_v[0].T.astype(BF16),
    )


def _trunk(x, mem, p):
    bsz, t, d = x.shape
    n = bsz * t
    lam_init = 0.8 - 0.6 * math.exp(-0.3 * 0)
    x2 = x.reshape(n, d)
    proj = _rms_matmul(x2, p["g_mix"], p["w_main"], BF16, 1024, 2048).reshape(bsz, t, MAIN_COLS)
    lr = _rms_matmul(x2, p["g_mix"], p["w_lr"], F32, 1024, LANES).reshape(bsz, t, LANES)
    kv = _rms_matmul(mem.reshape(bsz * N_MEM, d), p["mem_g_norm"], p["mem_w_kv"], F32, 512, 2 * MEM_Q)
    kv = kv.reshape(bsz, N_MEM, 2 * MEM_Q)
    pre_gla = _gla_branch(proj, lr, *p["gla"], tb=256)
    pre_diff = _diff_branch(proj, *p["diff"], lam_init=lam_init, tm=512, tq=512, tk=512)
    pre_mem = _mem_branch(proj, kv, p["mem_gq"], p["mem_gk"], tm=512)
    x1 = _merge(x2, proj.reshape(n, MAIN_COLS), pre_gla.reshape(n, d), pre_diff.reshape(n, d),
                pre_mem.reshape(n, d), *p["w_o"], tm=512)
    y = _peer(x1, p["g_ffn"], p["wqt"], *p["keys"], p["u"], p["vt"], tt=512)
    return y.reshape(bsz, t, d)


def kernel(x_prompt, x_sample, mem_prompt, mem_sample, g_mix, w_in, gla_w_dec_f, gla_b_dec_f, gla_w_dec_b, gla_b_dec_b, gla_g_out, gla_w_o, diff_g_q, diff_g_k, diff_lq1, diff_lk1, diff_lq2, diff_lk2, diff_g_sub, diff_w_o, mem_g_norm, mem_w_kv, mem_g_q, mem_g_k, mem_w_o, w_out, g_ffn, peer_w_q, peer_sub_k1, peer_sub_k2, peer_u, peer_v):
    p = _prepare_weights(g_mix, w_in, gla_w_dec_f, gla_b_dec_f, gla_w_dec_b, gla_b_dec_b, gla_g_out, gla_w_o,
                         diff_g_q, diff_g_k, diff_lq1, diff_lk1, diff_lq2, diff_lk2, diff_g_sub, diff_w_o,
                         mem_g_norm, mem_w_kv, mem_g_q, mem_g_k, mem_w_o, w_out,
                         g_ffn, peer_w_q, peer_sub_k1, peer_sub_k2, peer_u, peer_v)
    return (_trunk(x_prompt, mem_prompt, p), _trunk(x_sample, mem_sample, p))
```

```python
import functools
import math

import jax
import jax.numpy as jnp
from jax import lax
from jax.experimental import pallas as pl
from jax.experimental.pallas import tpu as pltpu

F32 = jnp.float32
BF16 = jnp.bfloat16

D_MODEL = 1024
N_MEM = 256
RMS_EPS = 1e-6
ROPE_THETA = 10000.0

GLA_HEADS = 4
GLA_DK = 128
GLA_DV = 256
GLA_LOWRANK = 16
GLA_GATE_NORM = 16.0
GLA_QK = GLA_HEADS * GLA_DK
GLA_V = GLA_HEADS * GLA_DV

DIFF_HEADS = 8
DIFF_DH = 64
DIFF_DV = 2 * DIFF_DH
DIFF_QK = DIFF_HEADS * 2 * DIFF_DH
DIFF_V = DIFF_HEADS * DIFF_DV

MEM_HEADS = 4
MEM_DH = 256
MEM_Q = MEM_HEADS * MEM_DH

N_BRANCH = 3
PEER_HEADS = 8
PEER_NKEYS = 128
PEER_EXPERTS = PEER_NKEYS * PEER_NKEYS
PEER_DQ = 256
PEER_DQH = PEER_DQ // 2
PEER_TOPK = 16

LANES = 128
SUBLANES = 8
VMEM_LIMIT = 56 * 1024 * 1024

OFF_GQ = 0
OFF_GK = OFF_GQ + GLA_QK
OFF_GV = OFF_GK + GLA_QK
OFF_GG = OFF_GV + GLA_V
OFF_DQ = OFF_GG + GLA_V
OFF_DK = OFF_DQ + DIFF_QK
OFF_DV = OFF_DK + DIFF_QK
OFF_MQ = OFF_DV + DIFF_V
OFF_BR = OFF_MQ + MEM_Q
MAIN_COLS = OFF_BR + N_BRANCH * D_MODEL

LOG2E = 1.4426950408889634


def _dot(a, b):
    return jnp.dot(a, b, preferred_element_type=F32)


def _dot_nt(a, b):
    return lax.dot_general(a, b, (((1,), (1,)), ((), ())), preferred_element_type=F32)


def _params(*sem):
    return pltpu.CompilerParams(dimension_semantics=sem, vmem_limit_bytes=VMEM_LIMIT)


def _rms_matmul_kernel(x_ref, g_ref, w_ref, o_ref, h_ref):
    @pl.when(pl.program_id(1) == 0)
    def _():
        x = x_ref[...]
        ms = jnp.mean(x * x, axis=-1, keepdims=True)
        h_ref[...] = (x * lax.rsqrt(ms + RMS_EPS) * g_ref[...]).astype(BF16)

    o_ref[...] = _dot(h_ref[...], w_ref[...]).astype(o_ref.dtype)


def _rms_matmul(x, g, w, out_dtype, tm, tn):
    n, d = x.shape
    m = w.shape[1]
    tm = min(tm, n)
    tn = min(tn, m)
    return pl.pallas_call(
        _rms_matmul_kernel,
        grid=(n // tm, m // tn),
        in_specs=[pl.BlockSpec((tm, d), lambda i, j: (i, 0)),
                  pl.BlockSpec((1, d), lambda i, j: (0, 0)),
                  pl.BlockSpec((d, tn), lambda i, j: (0, j))],
        out_specs=pl.BlockSpec((tm, tn), lambda i, j: (i, j)),
        out_shape=jax.ShapeDtypeStruct((n, m), out_dtype),
        scratch_shapes=[pltpu.VMEM((tm, d), BF16)],
        compiler_params=_params("parallel", "arbitrary"),
    )(x, g.reshape(1, d), w)


GLA_CHUNK = 128


def _gla_kernel(*refs, reverse, nchunk, final):
    if final:
        (q_ref, k_ref, v_ref, lr_ref, wd_ref, bd_ref, of_ref, gate_ref, gout_ref, o_ref, st_ref) = refs
    else:
        (q_ref, k_ref, v_ref, lr_ref, wd_ref, bd_ref, o_ref, st_ref) = refs
    C = GLA_CHUNK

    @pl.when(pl.program_id(1) == 0)
    def _():
        st_ref[...] = jnp.zeros_like(st_ref)

    rows = lax.broadcasted_iota(jnp.int32, (C, C), 0)
    cols = lax.broadcasted_iota(jnp.int32, (C, C), 1)
    tri = jnp.where(cols <= rows, 1.0, 0.0).astype(BF16)
    keep = (cols >= rows) if reverse else (cols <= rows)
    scale = GLA_DK ** -0.5
    order = range(nchunk - 1, -1, -1) if reverse else range(nchunk)
    for ci in order:
        sl = pl.ds(ci * C, C)
        lr_c = lr_ref[0, sl, :].astype(BF16)
        for h in range(GLA_HEADS):
            dk = slice(h * GLA_DK, (h + 1) * GLA_DK)
            dv = slice(h * GLA_DV, (h + 1) * GLA_DV)
            q = q_ref[0, sl, dk].astype(F32) * scale
            k = k_ref[0, sl, dk].astype(F32)
            v = v_ref[0, sl, dv]
            pre = _dot(lr_c, wd_ref[h]) + bd_ref[h]
            la = (jnp.minimum(pre, 0.0) - jnp.log1p(jnp.exp(-jnp.abs(pre)))) * (1.0 / GLA_GATE_NORM)
            la_hi = la.astype(BF16)
            la_lo = (la - la_hi.astype(F32)).astype(BF16)
            b = _dot(tri, la_hi) + _dot(tri, la_lo)
            tot = b[C - 1:C, :]
            st = st_ref[h]
            vt = v.astype(F32).T.astype(BF16)
            if not reverse:
                q_in = (q * jnp.exp(b)).astype(BF16)
                q_st = q_in
                k_in = (k * jnp.exp(-b)).astype(BF16)
                k_st = (k * jnp.exp(tot - b)).astype(BF16)
            else:
                c = b - la
                q_in = (q * jnp.exp(-c)).astype(BF16)
                q_st = (q * jnp.exp(tot - c)).astype(BF16)
                k_in = (k * jnp.exp(c)).astype(BF16)
                k_st = k_in
            att = jnp.where(keep, _dot_nt(q_in, k_in), 0.0)
            o = _dot(att.astype(BF16), v) + _dot_nt(q_st, st.astype(BF16))
            st_ref[h] = jnp.exp(tot) * st + _dot(vt, k_st)
            if final:
                diag = jnp.sum(q * k, axis=-1, keepdims=True)
                o = of_ref[0, sl, dv] + o - diag * v.astype(F32)
                ms = jnp.mean(o * o, axis=-1, keepdims=True)
                o = o * lax.rsqrt(ms + RMS_EPS) * gout_ref[...]
                gt = gate_ref[0, sl, dv].astype(F32)
                o = o * (gt * jax.nn.sigmoid(gt))
            o_ref[0, sl, dv] = o.astype(o_ref.dtype)


def _gla_branch(proj, lr, w_dec_f, b_dec_f, w_dec_b, b_dec_b, g_out, tb):
    bsz, t, _ = proj.shape
    tb = min(tb, t)
    nblk = t // tb
    nchunk = tb // GLA_CHUNK

    def dec_weights(w_dec, b_dec, row0):
        w = jnp.zeros((GLA_HEADS, LANES, GLA_DK), F32)
        w = w.at[:, row0:row0 + GLA_LOWRANK, :].set(
            w_dec.reshape(GLA_LOWRANK, GLA_HEADS, GLA_DK).transpose(1, 0, 2))
        return w.astype(BF16), b_dec.reshape(GLA_HEADS, 1, GLA_DK).astype(F32)

    def call(reverse, o_fwd):
        wd, bd = dec_weights(w_dec_b, b_dec_b, GLA_LOWRANK) if reverse else dec_weights(w_dec_f, b_dec_f, 0)
        blk = (lambda j: nblk - 1 - j) if reverse else (lambda j: j)
        in_specs = [
            pl.BlockSpec((1, tb, GLA_QK), lambda b, j: (b, blk(j), OFF_GQ // GLA_QK)),
            pl.BlockSpec((1, tb, GLA_QK), lambda b, j: (b, blk(j), OFF_GK // GLA_QK)),
            pl.BlockSpec((1, tb, GLA_V), lambda b, j: (b, blk(j), OFF_GV // GLA_V)),
            pl.BlockSpec((1, tb, LANES), lambda b, j: (b, blk(j), 0)),
            pl.BlockSpec((GLA_HEADS, LANES, GLA_DK), lambda b, j: (0, 0, 0)),
            pl.BlockSpec((GLA_HEADS, 1, GLA_DK), lambda b, j: (0, 0, 0)),
        ]
        args = [proj, proj, proj, lr, wd, bd]
        if reverse:
            in_specs += [
                pl.BlockSpec((1, tb, GLA_V), lambda b, j: (b, blk(j), 0)),
                pl.BlockSpec((1, tb, GLA_V), lambda b, j: (b, blk(j), OFF_GG // GLA_V)),
                pl.BlockSpec((1, GLA_DV), lambda b, j: (0, 0)),
            ]
            args += [o_fwd, proj, g_out.reshape(1, GLA_DV).astype(F32)]
        return pl.pallas_call(
            functools.partial(_gla_kernel, reverse=reverse, nchunk=nchunk, final=reverse),
            grid=(bsz, nblk),
            in_specs=in_specs,
            out_specs=pl.BlockSpec((1, tb, GLA_V), lambda b, j: (b, blk(j), 0)),
            out_shape=jax.ShapeDtypeStruct((bsz, t, GLA_V), BF16 if reverse else F32),
            scratch_shapes=[pltpu.VMEM((GLA_HEADS, GLA_DV, GLA_DK), F32)],
            compiler_params=_params("parallel", "arbitrary"),
        )(*args)

    return call(True, call(False, None))


def _group_sumsq(x, ones_bd):
    x2 = x * x
    hi = x2.astype(BF16)
    lo = (x2 - hi.astype(F32)).astype(BF16)
    pieces = []
    for c in range(x.shape[1] // LANES):
        cs = slice(c * LANES, (c + 1) * LANES)
        pieces.append(_dot(hi[:, cs], ones_bd) + _dot(lo[:, cs], ones_bd))
    return jnp.concatenate(pieces, axis=1)


def _diff_prep_kernel(q_ref, k_ref, gq_ref, gk_ref, cos_ref, sin_ref, qz_ref, kt_ref):
    tm = q_ref.shape[1]
    r = lax.broadcasted_iota(jnp.int32, (LANES, LANES), 0) // DIFF_DH
    c = lax.broadcasted_iota(jnp.int32, (LANES, LANES), 1) // DIFF_DH
    ones_bd = jnp.where(r == c, 1.0, 0.0).astype(BF16)
    lane = lax.broadcasted_iota(jnp.int32, (tm, DIFF_QK), 1)
    first_half = (lane % DIFF_DH) < (DIFF_DH // 2)
    reps = DIFF_QK // LANES
    cos = jnp.concatenate([cos_ref[...]] * reps, axis=1)
    sin = jnp.concatenate([sin_ref[...]] * reps, axis=1)

    def norm_rope(x, g):
        ms = _group_sumsq(x, ones_bd) * (1.0 / DIFF_DH)
        xn = x * lax.rsqrt(ms + RMS_EPS) * g
        partner = jnp.where(first_half,
                            pltpu.roll(xn, DIFF_QK - DIFF_DH // 2, axis=1),
                            pltpu.roll(xn, DIFF_DH // 2, axis=1))
        return xn * cos + partner * sin

    qr = norm_rope(q_ref[0].astype(F32), gq_ref[...]) * (DIFF_DH ** -0.5 * LOG2E)
    kr = norm_rope(k_ref[0].astype(F32), gk_ref[...])
    lane_h = lax.broadcasted_iota(jnp.int32, (tm, LANES), 1)
    pieces = []
    for h in range(DIFF_HEADS):
        qh = qr[:, h * LANES:(h + 1) * LANES]
        pieces.append(jnp.where(lane_h < DIFF_DH, qh, 0.0))
        pieces.append(jnp.where(lane_h >= DIFF_DH, qh, 0.0))
        kt_ref[0, h] = kr[:, h * LANES:(h + 1) * LANES].T.astype(BF16)
    qz_ref[0] = jnp.concatenate(pieces, axis=1).astype(BF16)


FLASH_SAFE_LOG2_RANGE = 60.0


def _flash_kernel(bound_ref, qz_ref, kt_ref, v_ref, lq1_ref, lk1_ref, lq2_ref, lk2_ref, gsub_ref, o_ref,
                  qs_ref, m_ref, l_ref, acc_ref, *, tq, tk, nk, lam_init):
    qs_ref[0:tq, :] = qz_ref[0, :, 0:LANES]
    qs_ref[tq:2 * tq, :] = qz_ref[0, :, LANES:2 * LANES]
    reps = tk // LANES
    bound = bound_ref[0]

    def scores(i):
        k0 = pl.multiple_of(i * tk, tk)
        s = _dot(qs_ref[...], kt_ref[0, 0, :, pl.ds(k0, tk)])
        return s, v_ref[0, pl.ds(k0, tk), :]

    def finalize(o):
        lam = (jnp.exp(jnp.sum(lq1_ref[...] * lk1_ref[...], axis=1, keepdims=True))
               - jnp.exp(jnp.sum(lq2_ref[...] * lk2_ref[...], axis=1, keepdims=True)) + lam_init)
        o = o[0:tq] - lam * o[tq:2 * tq]
        ms = jnp.mean(o * o, axis=-1, keepdims=True)
        o = o * lax.rsqrt(ms + RMS_EPS) * gsub_ref[...] * (1.0 - lam_init)
        o_ref[0] = o.astype(o_ref.dtype)

    @pl.when(bound <= FLASH_SAFE_LOG2_RANGE)
    def _():
        l_ref[...] = jnp.zeros_like(l_ref)
        acc_ref[...] = jnp.zeros_like(acc_ref)

        def step(i, carry):
            s, v = scores(i)
            p = jnp.exp2(s - bound)
            part = p[:, 0:LANES]
            for r in range(1, reps):
                part = part + p[:, r * LANES:(r + 1) * LANES]
            l_ref[...] += part
            acc_ref[...] += _dot(p.astype(BF16), v)
            return carry

        lax.fori_loop(0, nk, step, 0)
        finalize(acc_ref[...] / jnp.sum(l_ref[...], axis=1, keepdims=True))

    @pl.when(bound > FLASH_SAFE_LOG2_RANGE)
    def _():
        m_ref[...] = jnp.full_like(m_ref, -jnp.inf)
        l_ref[...] = jnp.zeros_like(l_ref)
        acc_ref[...] = jnp.zeros_like(acc_ref)

        def step(i, carry):
            s, v = scores(i)
            m_prev = m_ref[...]
            m_new = jnp.maximum(m_prev, jnp.max(s, axis=1, keepdims=True))
            alpha = jnp.exp2(m_prev - m_new)
            p = jnp.exp2(s - jnp.concatenate([m_new] * reps, axis=1))
            l_ref[...] = alpha * l_ref[...] + jnp.sum(p, axis=1, keepdims=True)
            acc_ref[...] = alpha * acc_ref[...] + _dot(p.astype(BF16), v)
            m_ref[...] = m_new
            return carry

        lax.fori_loop(0, nk, step, 0)
        finalize(acc_ref[...] / l_ref[...])


def _rope_tables(t):
    half = DIFF_DH // 2
    inv_freq = ROPE_THETA ** (-jnp.arange(0, DIFF_DH, 2, dtype=F32) / DIFF_DH)
    ang = jnp.arange(t, dtype=F32)[:, None] * inv_freq[None, :]
    cos, sin = jnp.cos(ang), jnp.sin(ang)
    cos_t = jnp.tile(cos, (1, LANES // half))
    sin_t = jnp.tile(jnp.concatenate([-sin, sin], axis=1), (1, LANES // DIFF_DH))
    return cos_t, sin_t


def _diff_branch(proj, g_q, g_k, lq1, lk1, lq2, lk2, g_sub, lam_init, tm, tq, tk):
    bsz, t, _ = proj.shape
    tm, tq, tk = min(tm, t), min(tq, t), min(tk, t)
    cos_t, sin_t = _rope_tables(t)
    tile_g = lambda g: jnp.tile(g.astype(F32), DIFF_QK // DIFF_DH).reshape(1, DIFF_QK)
    qz, kt = pl.pallas_call(
        _diff_prep_kernel,
        grid=(bsz, t // tm),
        in_specs=[pl.BlockSpec((1, tm, DIFF_QK), lambda b, i: (b, i, OFF_DQ // DIFF_QK)),
                  pl.BlockSpec((1, tm, DIFF_QK), lambda b, i: (b, i, OFF_DK // DIFF_QK)),
                  pl.BlockSpec((1, DIFF_QK), lambda b, i: (0, 0)),
                  pl.BlockSpec((1, DIFF_QK), lambda b, i: (0, 0)),
                  pl.BlockSpec((tm, LANES), lambda b, i: (i, 0)),
                  pl.BlockSpec((tm, LANES), lambda b, i: (i, 0))],
        out_specs=[pl.BlockSpec((1, tm, 2 * DIFF_QK), lambda b, i: (b, i, 0)),
                   pl.BlockSpec((1, DIFF_HEADS, LANES, tm), lambda b, i: (b, 0, 0, i))],
        out_shape=[jax.ShapeDtypeStruct((bsz, t, 2 * DIFF_QK), BF16),
                   jax.ShapeDtypeStruct((bsz, DIFF_HEADS, LANES, t), BF16)],
        compiler_params=_params("parallel", "parallel"),
    )(proj, proj, tile_g(g_q), tile_g(g_k), cos_t, sin_t)

    vec = lambda a: a.reshape(1, DIFF_DH).astype(F32)
    bound = (1.01 * DIFF_DH * (DIFF_DH ** -0.5 * LOG2E)
             * jnp.max(jnp.abs(g_q)) * jnp.max(jnp.abs(g_k))).astype(F32).reshape(1)
    return pl.pallas_call(
        functools.partial(_flash_kernel, tq=tq, tk=tk, nk=t // tk, lam_init=lam_init),
        grid=(bsz, DIFF_HEADS, t // tq),
        in_specs=[pl.BlockSpec(memory_space=pltpu.SMEM),
                  pl.BlockSpec((1, tq, 2 * LANES), lambda b, h, i: (b, i, h)),
                  pl.BlockSpec((1, 1, LANES, t), lambda b, h, i: (b, h, 0, 0)),
                  pl.BlockSpec((1, t, DIFF_DV), lambda b, h, i: (b, 0, OFF_DV // DIFF_DV + h)),
                  pl.BlockSpec((1, DIFF_DH), lambda b, h, i: (0, 0)),
                  pl.BlockSpec((1, DIFF_DH), lambda b, h, i: (0, 0)),
                  pl.BlockSpec((1, DIFF_DH), lambda b, h, i: (0, 0)),
                  pl.BlockSpec((1, DIFF_DH), lambda b, h, i: (0, 0)),
                  pl.BlockSpec((1, DIFF_DV), lambda b, h, i: (0, 0))],
        out_specs=pl.BlockSpec((1, tq, DIFF_DV), lambda b, h, i: (b, i, h)),
        out_shape=jax.ShapeDtypeStruct((bsz, t, DIFF_V), BF16),
        scratch_shapes=[pltpu.VMEM((2 * tq, LANES), BF16),
                        pltpu.VMEM((2 * tq, LANES), F32),
                        pltpu.VMEM((2 * tq, LANES), F32),
                        pltpu.VMEM((2 * tq, DIFF_DV), F32)],
        compiler_params=_params("parallel", "parallel", "arbitrary"),
    )(bound, qz, kt, proj, vec(lq1), vec(lk1), vec(lq2), vec(lk2), g_sub.reshape(1, DIFF_DV).astype(F32))


def _mem_kernel(q_ref, kv_ref, gq_ref, gk_ref, o_ref):
    outs = []
    for h in range(MEM_HEADS):
        hs = slice(h * MEM_DH, (h + 1) * MEM_DH)
        q = q_ref[0, :, hs].astype(F32)
        q = q * lax.rsqrt(jnp.mean(q * q, axis=-1, keepdims=True) + RMS_EPS) * gq_ref[...]
        k = kv_ref[0, :, hs]
        k = k * lax.rsqrt(jnp.mean(k * k, axis=-1, keepdims=True) + RMS_EPS) * gk_ref[...]
        v = kv_ref[0, :, MEM_Q + h * MEM_DH:MEM_Q + (h + 1) * MEM_DH]
        s = _dot_nt(q.astype(BF16), k.astype(BF16)) * (MEM_DH ** -0.5)
        p = jnp.exp(s - jnp.max(s, axis=-1, keepdims=True))
        p = p / jnp.sum(p, axis=-1, keepdims=True)
        outs.append(_dot(p.astype(BF16), v.astype(BF16)))
    o_ref[0] = jnp.concatenate(outs, axis=1).astype(o_ref.dtype)


def _mem_branch(proj, kv, g_q, g_k, tm):
    bsz, t, _ = proj.shape
    tm = min(tm, t)
    return pl.pallas_call(
        _mem_kernel,
        grid=(bsz, t // tm),
        in_specs=[pl.BlockSpec((1, tm, MEM_Q), lambda b, i: (b, i, OFF_MQ // MEM_Q)),
                  pl.BlockSpec((1, N_MEM, 2 * MEM_Q), lambda b, i: (b, 0, 0)),
                  pl.BlockSpec((1, MEM_DH), lambda b, i: (0, 0)),
                  pl.BlockSpec((1, MEM_DH), lambda b, i: (0, 0))],
        out_specs=pl.BlockSpec((1, tm, MEM_Q), lambda b, i: (b, i, 0)),
        out_shape=jax.ShapeDtypeStruct((bsz, t, MEM_Q), BF16),
        compiler_params=_params("parallel", "parallel"),
    )(proj, kv, g_q.reshape(1, MEM_DH).astype(F32), g_k.reshape(1, MEM_DH).astype(F32))


def _merge_kernel(x_ref, pg_ref, pd_ref, pm_ref, g0_ref, g1_ref, g2_ref,
                  wg_ref, wd_ref, wm_ref, wo_ref, o_ref):
    merged = (jax.nn.sigmoid(g0_ref[...].astype(F32)) * _dot(pg_ref[...], wg_ref[...])
              + jax.nn.sigmoid(g1_ref[...].astype(F32)) * _dot(pd_ref[...], wd_ref[...])
              + jax.nn.sigmoid(g2_ref[...].astype(F32)) * _dot(pm_ref[...], wm_ref[...]))
    o_ref[...] = x_ref[...] + _dot(merged.astype(BF16), wo_ref[...])


def _merge(x, proj, pg, pd, pm, wg, wd, wm, wo, tm):
    n, d = x.shape
    tm = min(tm, n)
    row = lambda c: pl.BlockSpec((tm, d), lambda i: (i, c))
    full = pl.BlockSpec((d, d), lambda i: (0, 0))
    br = OFF_BR // d
    return pl.pallas_call(
        _merge_kernel,
        grid=(n // tm,),
        in_specs=[row(0), row(0), row(0), row(0), row(br), row(br + 1), row(br + 2), full, full, full, full],
        out_specs=row(0),
        out_shape=jax.ShapeDtypeStruct((n, d), F32),
        compiler_params=_params("parallel"),
    )(x, pg, pd, pm, proj, proj, proj, wg, wd, wm, wo)


PEER_ECHUNK = 1024
PEER_IPER = PEER_ECHUNK // PEER_NKEYS
PEER_IGROUP = 2
PEER_JBLK = 128


def _bitonic_merge_desc(a):
    n = len(a)
    j = n // 2
    while j >= 1:
        for i in range(n):
            l = i ^ j
            if l > i:
                a[i], a[l] = jnp.maximum(a[i], a[l]), jnp.minimum(a[i], a[l])
        j //= 2
    return a


def _top16_desc(vals):
    a = list(vals)
    n = len(a)
    k = 2
    while k <= n:
        j = k // 2
        while j >= 1:
            for i in range(n):
                l = i ^ j
                if l > i:
                    hi, lo = jnp.maximum(a[i], a[l]), jnp.minimum(a[i], a[l])
                    a[i], a[l] = (hi, lo) if (i & k) == 0 else (lo, hi)
            j //= 2
        k *= 2
    for shift in (4, 2, 1):
        a = [jnp.maximum(a[r], pltpu.roll(a[n - 1 - r], shift, axis=0)) for r in range(n)]
        a = _bitonic_merge_desc(a)
    return a


def _dup_bf16_words(x):
    bits = lax.bitcast_convert_type(x.astype(BF16).astype(F32), jnp.uint32)
    return lax.bitcast_convert_type(bits | (bits >> 16), F32)


def _packed_row(row, nrows):
    return pltpu.bitcast(jnp.broadcast_to(row, (nrows // 2, row.shape[1])), BF16)


def _peer_kernel(x_ref, g_ref, wqt_ref, k1h_ref, k1l_ref, k2h_ref, k2l_ref, u_ref, vt_ref, o_ref,
                 hbt_ref, t_ref, c_ref, e2_ref, a_ref, wt_ref, acc_ref):
    ec = pl.program_id(1)
    tt = x_ref.shape[0]
    neg = jnp.full((SUBLANES, tt), -jnp.inf, F32)
    sub = lax.broadcasted_iota(jnp.int32, (SUBLANES, tt), 0)

    @pl.when(ec == 0)
    def _():
        x = x_ref[...]
        hb = x * lax.rsqrt(jnp.mean(x * x, axis=-1, keepdims=True) + RMS_EPS) * g_ref[...]
        hbt_ref[...] = hb.T.astype(BF16)
        acc_ref[...] = jnp.zeros_like(acc_ref)
        for h in range(PEER_HEADS):
            def scores(kh_ref, kl_ref, row0):
                qt = _dot(wqt_ref[row0:row0 + PEER_DQH, :], hbt_ref[...])
                qh = qt.astype(BF16)
                ql = (qt - qh.astype(F32)).astype(BF16)
                return _dot(kh_ref[...], qh) + _dot(kh_ref[...], ql) + _dot(kl_ref[...], qh)

            s1 = scores(k1h_ref, k1l_ref, h * PEER_DQ)
            s2 = scores(k2h_ref, k2l_ref, h * PEER_DQ + PEER_DQH)
            blocks = lambda s: [s[r * SUBLANES:(r + 1) * SUBLANES, :] for r in range(PEER_NKEYS // SUBLANES)]
            v1 = _top16_desc(blocks(s1))
            v2 = _top16_desc(blocks(s2))

            def pack(vs):
                out = neg
                for r in range(SUBLANES):
                    out = jnp.where(sub == r, vs[r], out)
                return out

            v2_lo, v2_hi, v1_hi = pack(v2[:SUBLANES]), pack(v2[SUBLANES:]), pack(v1[SUBLANES:])
            cand = [v1[0] + v2_lo, v1[0] + v2_hi, v1_hi + v2[0]]
            cand += [v1[a] + v2_lo for a in range(1, SUBLANES)]
            cand += [neg] * (PEER_TOPK - len(cand))
            top = _top16_desc(cand)
            tau16 = top[PEER_TOPK - 1]
            below = lambda s, bound: jnp.max(jnp.where(s < bound, s, -jnp.inf), axis=0, keepdims=True)
            tau17 = below(jnp.concatenate(cand[:10], axis=0), tau16[0:1, :])
            tau17 = jnp.maximum(tau17, below(s1, v1[PEER_TOPK - 1][0:1, :]) + v2[0][0:1, :])
            tau17 = jnp.maximum(tau17, below(s2, v2[PEER_TOPK - 1][0:1, :]) + v1[0][0:1, :])
            tau = 0.5 * (tau16[0:1, :] + tau17)
            z = top[0] - top[0]
            for r in range(PEER_TOPK):
                z = z + jnp.exp(top[r] - top[0])
            z = z[0:1, :]
            nchunks = PEER_NKEYS // PEER_IPER
            m2 = v2[0][0:1, :]
            t_ref[h] = _dup_bf16_words(jnp.exp(tau - s1 - m2)).reshape(nchunks, PEER_IPER, tt)
            c_ref[h] = _dup_bf16_words(0.5 * jnp.exp(s1 - v1[0][0:1, :]) / z).reshape(nchunks, PEER_IPER, tt)
            e2_ref[h] = jnp.exp(s2 - m2).astype(BF16)

    rows_per_piece = PEER_IGROUP * PEER_NKEYS
    nhalf = 2 if tt % (2 * LANES) == 0 else 1
    lanes_per_half = tt // nhalf
    for ig, half in [(ig, half) for half in range(nhalf) for ig in range(PEER_IPER // PEER_IGROUP)]:
        ps = slice(ig * rows_per_piece, (ig + 1) * rows_per_piece)
        if half == 0:
            a_ref[ps, :] = _dot(u_ref[ps, :], hbt_ref[...])
        for lb in range(half * lanes_per_half // LANES, (half + 1) * lanes_per_half // LANES):
            ls = slice(lb * LANES, (lb + 1) * LANES)
            for jb in range(PEER_NKEYS // PEER_JBLK):
                js = slice(jb * PEER_JBLK, (jb + 1) * PEER_JBLK)
                accs = [None] * PEER_IGROUP
                for h in range(PEER_HEADS):
                    e2v = e2_ref[h, js, ls]
                    for q in range(PEER_IGROUP):
                        il = ig * PEER_IGROUP + q
                        thr = _packed_row(t_ref[h, ec, il:il + 1, ls], PEER_JBLK)
                        coef = _packed_row(c_ref[h, ec, il:il + 1, ls], PEER_JBLK)
                        term = jnp.where(e2v >= thr, e2v, jnp.zeros_like(e2v)) * coef
                        accs[q] = term if accs[q] is None else accs[q] + term
                for q in range(PEER_IGROUP):
                    il = ig * PEER_IGROUP + q
                    rs = slice(il * PEER_NKEYS + jb * PEER_JBLK, il * PEER_NKEYS + (jb + 1) * PEER_JBLK)
                    a = a_ref[rs, ls]
                    gelu2 = a * (1.0 + lax.erf(a * (2.0 ** -0.5)))
                    wt_ref[rs, ls] = gelu2.astype(BF16) * accs[q]
        if ig == PEER_IPER // PEER_IGROUP - 1:
            hs = slice(half * lanes_per_half, (half + 1) * lanes_per_half)
            acc_ref[:, hs] += _dot(vt_ref[...], wt_ref[:, hs])

    @pl.when(ec == pl.num_programs(1) - 1)
    def _():
        o_ref[...] = x_ref[...] + acc_ref[...].T


def _peer(x, g_ffn, wqt, k1h, k1l, k2h, k2l, u, vt, tt):
    n, d = x.shape
    tt = min(tt, n)
    nec = PEER_EXPERTS // PEER_ECHUNK
    const = lambda shape: pl.BlockSpec(shape, lambda i, e: (0,) * len(shape))
    key_spec = const((PEER_NKEYS, PEER_DQH))
    stat = pltpu.VMEM((PEER_HEADS, PEER_NKEYS, tt), F32)
    stat_i = pltpu.VMEM((PEER_HEADS, PEER_NKEYS // PEER_IPER, PEER_IPER, tt), F32)
    return pl.pallas_call(
        _peer_kernel,
        grid=(n // tt, nec),
        in_specs=[pl.BlockSpec((tt, d), lambda i, e: (i, 0)),
                  const((1, d)),
                  const((PEER_HEADS * PEER_DQ, d)),
                  key_spec, key_spec, key_spec, key_spec,
                  pl.BlockSpec((PEER_ECHUNK, d), lambda i, e: (e, 0)),
                  pl.BlockSpec((d, PEER_ECHUNK), lambda i, e: (0, e))],
        out_specs=pl.BlockSpec((tt, d), lambda i, e: (i, 0)),
        out_shape=jax.ShapeDtypeStruct((n, d), F32),
        scratch_shapes=[pltpu.VMEM((d, tt), BF16), stat_i, stat_i,
                        pltpu.VMEM((PEER_HEADS, PEER_NKEYS, tt), BF16),
                        pltpu.VMEM((PEER_ECHUNK, tt), F32),
                        pltpu.VMEM((PEER_ECHUNK, tt), BF16),
                        pltpu.VMEM((d, tt), F32)],
        compiler_params=_params("parallel", "arbitrary"),
    )(x, g_ffn.reshape(1, d), wqt, k1h, k1l, k2h, k2l, u, vt)


def _split_hi_lo(a):
    hi = a.astype(BF16)
    return hi, (a - hi.astype(F32)).astype(BF16)


def _prepare_weights(g_mix, w_in, gla_w_dec_f, gla_b_dec_f, gla_w_dec_b, gla_b_dec_b, gla_g_out, gla_w_o,
                     diff_g_q, diff_g_k, diff_lq1, diff_lk1, diff_lq2, diff_lk2, diff_g_sub, diff_w_o,
                     mem_g_norm, mem_w_kv, mem_g_q, mem_g_k, mem_w_o, w_out,
                     g_ffn, peer_w_q, peer_sub_k1, peer_sub_k2, peer_u, peer_v):
    w = w_in[0]
    lr0 = 2 * GLA_QK + 2 * GLA_V
    lr1 = lr0 + 2 * GLA_LOWRANK
    w_main = jnp.concatenate([w[:, :lr0], w[:, lr1:]], axis=1).astype(BF16)
    w_lr = jnp.pad(w[:, lr0:lr1], ((0, 0), (0, LANES - 2 * GLA_LOWRANK))).astype(BF16)
    k1h, k1l = _split_hi_lo(peer_sub_k1[0])
    k2h, k2l = _split_hi_lo(peer_sub_k2[0])
    return dict(
        g_mix=g_mix[0], w_main=w_main, w_lr=w_lr,
        gla=(gla_w_dec_f[0], gla_b_dec_f[0], gla_w_dec_b[0], gla_b_dec_b[0], gla_g_out[0]),
        diff=(diff_g_q[0], diff_g_k[0], diff_lq1[0], diff_lk1[0], diff_lq2[0], diff_lk2[0], diff_g_sub[0]),
        mem_g_norm=mem_g_norm[0], mem_w_kv=mem_w_kv[0].astype(BF16), mem_gq=mem_g_q[0], mem_gk=mem_g_k[0],
        w_o=(gla_w_o[0].astype(BF16), diff_w_o[0].astype(BF16), mem_w_o[0].astype(BF16), w_out[0].astype(BF16)),
        g_ffn=g_ffn[0], wqt=peer_w_q[0].T.astype(BF16), keys=(k1h, k1l, k2h, k2l),
        u=peer_u[0].astype(BF16), vt=peer_v[0].T.astype(BF16),
    )


def _trunk(x, mem, p):
    bsz, t, d = x.shape
    n = bsz * t
    lam_init = 0.8 - 0.6 * math.exp(-0.3 * 0)
    x2 = x.reshape(n, d)
    proj = _rms_matmul(x2, p["g_mix"], p["w_main"], BF16, 1024, 2048).reshape(bsz, t, MAIN_COLS)
    lr = _rms_matmul(x2, p["g_mix"], p["w_lr"], F32, 1024, LANES).reshape(bsz, t, LANES)
    kv = _rms_matmul(mem.reshape(bsz * N_MEM, d), p["mem_g_norm"], p["mem_w_kv"], F32, 512, 2 * MEM_Q)
    kv = kv.reshape(bsz, N_MEM, 2 * MEM_Q)
    pre_gla = _gla_branch(proj, lr, *p["gla"], tb=256)
    pre_diff = _diff_branch(proj, *p["diff"], lam_init=lam_init, tm=512, tq=512, tk=1024)
    pre_mem = _mem_branch(proj, kv, p["mem_gq"], p["mem_gk"], tm=512)
    x1 = _merge(x2, proj.reshape(n, MAIN_COLS), pre_gla.reshape(n, d), pre_diff.reshape(n, d),
                pre_mem.reshape(n, d), *p["w_o"], tm=512)
    y = _peer(x1, p["g_ffn"], p["wqt"], *p["keys"], p["u"], p["vt"], tt=512)
    return y.reshape(bsz, t, d)


def kernel(x_prompt, x_sample, mem_prompt, mem_sample, g_mix, w_in, gla_w_dec_f, gla_b_dec_f, gla_w_dec_b, gla_b_dec_b, gla_g_out, gla_w_o, diff_g_q, diff_g_k, diff_lq1, diff_lk1, diff_lq2, diff_lk2, diff_g_sub, diff_w_o, mem_g_norm, mem_w_kv, mem_g_q, mem_g_k, mem_w_o, w_out, g_ffn, peer_w_q, peer_sub_k1, peer_sub_k2, peer_u, peer_v):
    p = _prepare_weights(g_mix, w_in, gla_w_dec_f, gla_b_dec_f, gla_w_dec_b, gla_b_dec_b, gla_g_out, gla_w_o,
                         diff_g_q, diff_g_k, diff_lq1, diff_lk1, diff_lq2, diff_lk2, diff_g_sub, diff_w_o,
                         mem_g_norm, mem_w_kv, mem_g_q, mem_g_k, mem_w_o, w_out,
                         g_ffn, peer_w_q, peer_sub_k1, peer_sub_k2, peer_u, peer_v)
    return (_trunk(x_prompt, mem_prompt, p), _trunk(x_sample, mem_sample, p))
```

```python
import functools
import math

import jax
import jax.numpy as jnp
from jax import lax
from jax.experimental import pallas as pl
from jax.experimental.pallas import tpu as pltpu

F32 = jnp.float32
BF16 = jnp.bfloat16

D_MODEL = 1024
N_MEM = 256
RMS_EPS = 1e-6
ROPE_THETA = 10000.0

GLA_HEADS = 4
GLA_DK = 128
GLA_DV = 256
GLA_LOWRANK = 16
GLA_GATE_NORM = 16.0
GLA_QK = GLA_HEADS * GLA_DK
GLA_V = GLA_HEADS * GLA_DV

DIFF_HEADS = 8
DIFF_DH = 64
DIFF_DV = 2 * DIFF_DH
DIFF_QK = DIFF_HEADS * 2 * DIFF_DH
DIFF_V = DIFF_HEADS * DIFF_DV

MEM_HEADS = 4
MEM_DH = 256
MEM_Q = MEM_HEADS * MEM_DH

N_BRANCH = 3
PEER_HEADS = 8
PEER_NKEYS = 128
PEER_EXPERTS = PEER_NKEYS * PEER_NKEYS
PEER_DQ = 256
PEER_DQH = PEER_DQ // 2
PEER_TOPK = 16

LANES = 128
SUBLANES = 8
VMEM_LIMIT = 56 * 1024 * 1024

OFF_GQ = 0
OFF_GK = OFF_GQ + GLA_QK
OFF_GV = OFF_GK + GLA_QK
OFF_GG = OFF_GV + GLA_V
OFF_DQ = OFF_GG + GLA_V
OFF_DK = OFF_DQ + DIFF_QK
OFF_DV = OFF_DK + DIFF_QK
OFF_MQ = OFF_DV + DIFF_V
OFF_BR = OFF_MQ + MEM_Q
MAIN_COLS = OFF_BR + N_BRANCH * D_MODEL

LOG2E = 1.4426950408889634


def _dot(a, b):
    return jnp.dot(a, b, preferred_element_type=F32)


def _dot_nt(a, b):
    return lax.dot_general(a, b, (((1,), (1,)), ((), ())), preferred_element_type=F32)


def _params(*sem):
    return pltpu.CompilerParams(dimension_semantics=sem, vmem_limit_bytes=VMEM_LIMIT)


def _rms_matmul_kernel(x_ref, g_ref, w_ref, o_ref, h_ref):
    @pl.when(pl.program_id(1) == 0)
    def _():
        x = x_ref[...]
        ms = jnp.mean(x * x, axis=-1, keepdims=True)
        h_ref[...] = (x * lax.rsqrt(ms + RMS_EPS) * g_ref[...]).astype(BF16)

    o_ref[...] = _dot(h_ref[...], w_ref[...]).astype(o_ref.dtype)


def _rms_matmul(x, g, w, out_dtype, tm, tn):
    n, d = x.shape
    m = w.shape[1]
    tm = min(tm, n)
    tn = min(tn, m)
    return pl.pallas_call(
        _rms_matmul_kernel,
        grid=(n // tm, m // tn),
        in_specs=[pl.BlockSpec((tm, d), lambda i, j: (i, 0)),
                  pl.BlockSpec((1, d), lambda i, j: (0, 0)),
                  pl.BlockSpec((d, tn), lambda i, j: (0, j))],
        out_specs=pl.BlockSpec((tm, tn), lambda i, j: (i, j)),
        out_shape=jax.ShapeDtypeStruct((n, m), out_dtype),
        scratch_shapes=[pltpu.VMEM((tm, d), BF16)],
        compiler_params=_params("parallel", "arbitrary"),
    )(x, g.reshape(1, d), w)


GLA_CHUNK = 128


def _gla_kernel(*refs, reverse, nchunk, final):
    if final:
        (q_ref, k_ref, v_ref, lr_ref, wd_ref, bd_ref, of_ref, gate_ref, gout_ref, o_ref, st_ref) = refs
    else:
        (q_ref, k_ref, v_ref, lr_ref, wd_ref, bd_ref, o_ref, st_ref) = refs
    C = GLA_CHUNK

    @pl.when(pl.program_id(1) == 0)
    def _():
        st_ref[...] = jnp.zeros_like(st_ref)

    rows = lax.broadcasted_iota(jnp.int32, (C, C), 0)
    cols = lax.broadcasted_iota(jnp.int32, (C, C), 1)
    tri = jnp.where(cols <= rows, 1.0, 0.0).astype(BF16)
    keep = (cols >= rows) if reverse else (cols <= rows)
    scale = GLA_DK ** -0.5
    order = range(nchunk - 1, -1, -1) if reverse else range(nchunk)
    for ci in order:
        sl = pl.ds(ci * C, C)
        lr_c = lr_ref[0, sl, :].astype(BF16)
        for h in range(GLA_HEADS):
            dk = slice(h * GLA_DK, (h + 1) * GLA_DK)
            dv = slice(h * GLA_DV, (h + 1) * GLA_DV)
            q = q_ref[0, sl, dk].astype(F32) * scale
            k = k_ref[0, sl, dk].astype(F32)
            v = v_ref[0, sl, dv]
            pre = _dot(lr_c, wd_ref[h]) + bd_ref[h]
            la = (jnp.minimum(pre, 0.0) - jnp.log1p(jnp.exp(-jnp.abs(pre)))) * (1.0 / GLA_GATE_NORM)
            la_hi = la.astype(BF16)
            la_lo = (la - la_hi.astype(F32)).astype(BF16)
            b = _dot(tri, la_hi) + _dot(tri, la_lo)
            tot = b[C - 1:C, :]
            st = st_ref[h]
            vt = v.astype(F32).T.astype(BF16)
            if not reverse:
                q_in = (q * jnp.exp(b)).astype(BF16)
                q_st = q_in
                k_in = (k * jnp.exp(-b)).astype(BF16)
                k_st = (k * jnp.exp(tot - b)).astype(BF16)
            else:
                c = b - la
                q_in = (q * jnp.exp(-c)).astype(BF16)
                q_st = (q * jnp.exp(tot - c)).astype(BF16)
                k_in = (k * jnp.exp(c)).astype(BF16)
                k_st = k_in
            att = jnp.where(keep, _dot_nt(q_in, k_in), 0.0)
            o = _dot(att.astype(BF16), v) + _dot_nt(q_st, st.astype(BF16))
            st_ref[h] = jnp.exp(tot) * st + _dot(vt, k_st)
            if final:
                diag = jnp.sum(q * k, axis=-1, keepdims=True)
                o = of_ref[0, sl, dv] + o - diag * v.astype(F32)
                ms = jnp.mean(o * o, axis=-1, keepdims=True)
                o = o * lax.rsqrt(ms + RMS_EPS) * gout_ref[...]
                gt = gate_ref[0, sl, dv].astype(F32)
                o = o * (gt * jax.nn.sigmoid(gt))
            o_ref[0, sl, dv] = o.astype(o_ref.dtype)


def _gla_branch(proj, lr, w_dec_f, b_dec_f, w_dec_b, b_dec_b, g_out, tb):
    bsz, t, _ = proj.shape
    tb = min(tb, t)
    nblk = t // tb
    nchunk = tb // GLA_CHUNK

    def dec_weights(w_dec, b_dec, row0):
        w = jnp.zeros((GLA_HEADS, LANES, GLA_DK), F32)
        w = w.at[:, row0:row0 + GLA_LOWRANK, :].set(
            w_dec.reshape(GLA_LOWRANK, GLA_HEADS, GLA_DK).transpose(1, 0, 2))
        return w.astype(BF16), b_dec.reshape(GLA_HEADS, 1, GLA_DK).astype(F32)

    def call(reverse, o_fwd):
        wd, bd = dec_weights(w_dec_b, b_dec_b, GLA_LOWRANK) if reverse else dec_weights(w_dec_f, b_dec_f, 0)
        blk = (lambda j: nblk - 1 - j) if reverse else (lambda j: j)
        in_specs = [
            pl.BlockSpec((1, tb, GLA_QK), lambda b, j: (b, blk(j), OFF_GQ // GLA_QK)),
            pl.BlockSpec((1, tb, GLA_QK), lambda b, j: (b, blk(j), OFF_GK // GLA_QK)),
            pl.BlockSpec((1, tb, GLA_V), lambda b, j: (b, blk(j), OFF_GV // GLA_V)),
            pl.BlockSpec((1, tb, LANES), lambda b, j: (b, blk(j), 0)),
            pl.BlockSpec((GLA_HEADS, LANES, GLA_DK), lambda b, j: (0, 0, 0)),
            pl.BlockSpec((GLA_HEADS, 1, GLA_DK), lambda b, j: (0, 0, 0)),
        ]
        args = [proj, proj, proj, lr, wd, bd]
        if reverse:
            in_specs += [
                pl.BlockSpec((1, tb, GLA_V), lambda b, j: (b, blk(j), 0)),
                pl.BlockSpec((1, tb, GLA_V), lambda b, j: (b, blk(j), OFF_GG // GLA_V)),
                pl.BlockSpec((1, GLA_DV), lambda b, j: (0, 0)),
            ]
            args += [o_fwd, proj, g_out.reshape(1, GLA_DV).astype(F32)]
        return pl.pallas_call(
            functools.partial(_gla_kernel, reverse=reverse, nchunk=nchunk, final=reverse),
            grid=(bsz, nblk),
            in_specs=in_specs,
            out_specs=pl.BlockSpec((1, tb, GLA_V), lambda b, j: (b, blk(j), 0)),
            out_shape=jax.ShapeDtypeStruct((bsz, t, GLA_V), BF16 if reverse else F32),
            scratch_shapes=[pltpu.VMEM((GLA_HEADS, GLA_DV, GLA_DK), F32)],
            compiler_params=_params("parallel", "arbitrary"),
        )(*args)

    return call(True, call(False, None))


def _group_sumsq(x, ones_bd):
    x2 = x * x
    hi = x2.astype(BF16)
    lo = (x2 - hi.astype(F32)).astype(BF16)
    pieces = []
    for c in range(x.shape[1] // LANES):
        cs = slice(c * LANES, (c + 1) * LANES)
        pieces.append(_dot(hi[:, cs], ones_bd) + _dot(lo[:, cs], ones_bd))
    return jnp.concatenate(pieces, axis=1)


def _diff_prep_kernel(q_ref, k_ref, v_ref, gq_ref, gk_ref, cos_ref, sin_ref, qz_ref, kt_ref, vh_ref):
    tm = q_ref.shape[1]
    r = lax.broadcasted_iota(jnp.int32, (LANES, LANES), 0) // DIFF_DH
    c = lax.broadcasted_iota(jnp.int32, (LANES, LANES), 1) // DIFF_DH
    ones_bd = jnp.where(r == c, 1.0, 0.0).astype(BF16)
    lane = lax.broadcasted_iota(jnp.int32, (tm, DIFF_QK), 1)
    first_half = (lane % DIFF_DH) < (DIFF_DH // 2)
    reps = DIFF_QK // LANES
    cos = jnp.concatenate([cos_ref[...]] * reps, axis=1)
    sin = jnp.concatenate([sin_ref[...]] * reps, axis=1)

    def norm_rope(x, g):
        ms = _group_sumsq(x, ones_bd) * (1.0 / DIFF_DH)
        xn = x * lax.rsqrt(ms + RMS_EPS) * g
        partner = jnp.where(first_half,
                            pltpu.roll(xn, DIFF_QK - DIFF_DH // 2, axis=1),
                            pltpu.roll(xn, DIFF_DH // 2, axis=1))
        return xn * cos + partner * sin

    qr = norm_rope(q_ref[0].astype(F32), gq_ref[...]) * (DIFF_DH ** -0.5 * LOG2E)
    kr = norm_rope(k_ref[0].astype(F32), gk_ref[...])
    lane_h = lax.broadcasted_iota(jnp.int32, (tm, LANES), 1)
    pieces = []
    for h in range(DIFF_HEADS):
        qh = qr[:, h * LANES:(h + 1) * LANES]
        pieces.append(jnp.where(lane_h < DIFF_DH, qh, 0.0))
        pieces.append(jnp.where(lane_h >= DIFF_DH, qh, 0.0))
        kt_ref[0, h] = kr[:, h * LANES:(h + 1) * LANES].T.astype(BF16)
        vh_ref[0, h] = v_ref[0, :, h * DIFF_DV:(h + 1) * DIFF_DV]
    qz_ref[0] = jnp.concatenate(pieces, axis=1).astype(BF16)


FLASH_SAFE_LOG2_RANGE = 60.0


def _flash_kernel(bound_ref, qz_ref, kt_ref, v_ref, lq1_ref, lk1_ref, lq2_ref, lk2_ref, gsub_ref, o_ref,
                  qs_ref, m_ref, l_ref, acc_ref, *, tq, tk, nk, lam_init):
    qs_ref[0:tq, :] = qz_ref[0, :, 0:LANES]
    qs_ref[tq:2 * tq, :] = qz_ref[0, :, LANES:2 * LANES]
    reps = tk // LANES
    bound = bound_ref[0]

    def scores(i):
        k0 = pl.multiple_of(i * tk, tk)
        s = _dot(qs_ref[...], kt_ref[0, 0, :, pl.ds(k0, tk)])
        return s, v_ref[0, 0, pl.ds(k0, tk), :]

    def finalize(o):
        lam = (jnp.exp(jnp.sum(lq1_ref[...] * lk1_ref[...], axis=1, keepdims=True))
               - jnp.exp(jnp.sum(lq2_ref[...] * lk2_ref[...], axis=1, keepdims=True)) + lam_init)
        o = o[0:tq] - lam * o[tq:2 * tq]
        ms = jnp.mean(o * o, axis=-1, keepdims=True)
        o = o * lax.rsqrt(ms + RMS_EPS) * gsub_ref[...] * (1.0 - lam_init)
        o_ref[0] = o.astype(o_ref.dtype)

    @pl.when(bound <= FLASH_SAFE_LOG2_RANGE)
    def _():
        l_ref[...] = jnp.zeros_like(l_ref)
        acc_ref[...] = jnp.zeros_like(acc_ref)

        def step(i, carry):
            s, v = scores(i)
            p = jnp.exp2(s - bound)
            part = p[:, 0:LANES]
            for r in range(1, reps):
                part = part + p[:, r * LANES:(r + 1) * LANES]
            l_ref[...] += part
            acc_ref[...] += _dot(p.astype(BF16), v)
            return carry

        lax.fori_loop(0, nk, step, 0)
        finalize(acc_ref[...] / jnp.sum(l_ref[...], axis=1, keepdims=True))

    @pl.when(bound > FLASH_SAFE_LOG2_RANGE)
    def _():
        m_ref[...] = jnp.full_like(m_ref, -jnp.inf)
        l_ref[...] = jnp.zeros_like(l_ref)
        acc_ref[...] = jnp.zeros_like(acc_ref)

        def step(i, carry):
            s, v = scores(i)
            m_prev = m_ref[...]
            m_new = jnp.maximum(m_prev, jnp.max(s, axis=1, keepdims=True))
            alpha = jnp.exp2(m_prev - m_new)
            p = jnp.exp2(s - jnp.concatenate([m_new] * reps, axis=1))
            l_ref[...] = alpha * l_ref[...] + jnp.sum(p, axis=1, keepdims=True)
            acc_ref[...] = alpha * acc_ref[...] + _dot(p.astype(BF16), v)
            m_ref[...] = m_new
            return carry

        lax.fori_loop(0, nk, step, 0)
        finalize(acc_ref[...] / l_ref[...])


def _rope_tables(t):
    half = DIFF_DH // 2
    inv_freq = ROPE_THETA ** (-jnp.arange(0, DIFF_DH, 2, dtype=F32) / DIFF_DH)
    ang = jnp.arange(t, dtype=F32)[:, None] * inv_freq[None, :]
    cos, sin = jnp.cos(ang), jnp.sin(ang)
    cos_t = jnp.tile(cos, (1, LANES // half))
    sin_t = jnp.tile(jnp.concatenate([-sin, sin], axis=1), (1, LANES // DIFF_DH))
    return cos_t, sin_t


def _diff_branch(proj, g_q, g_k, lq1, lk1, lq2, lk2, g_sub, lam_init, tm, tq, tk):
    bsz, t, _ = proj.shape
    tm, tq, tk = min(tm, t), min(tq, t), min(tk, t)
    cos_t, sin_t = _rope_tables(t)
    tile_g = lambda g: jnp.tile(g.astype(F32), DIFF_QK // DIFF_DH).reshape(1, DIFF_QK)
    qz, kt, vh = pl.pallas_call(
        _diff_prep_kernel,
        grid=(bsz, t // tm),
        in_specs=[pl.BlockSpec((1, tm, DIFF_QK), lambda b, i: (b, i, OFF_DQ // DIFF_QK)),
                  pl.BlockSpec((1, tm, DIFF_QK), lambda b, i: (b, i, OFF_DK // DIFF_QK)),
                  pl.BlockSpec((1, tm, DIFF_V), lambda b, i: (b, i, OFF_DV // DIFF_V)),
                  pl.BlockSpec((1, DIFF_QK), lambda b, i: (0, 0)),
                  pl.BlockSpec((1, DIFF_QK), lambda b, i: (0, 0)),
                  pl.BlockSpec((tm, LANES), lambda b, i: (i, 0)),
                  pl.BlockSpec((tm, LANES), lambda b, i: (i, 0))],
        out_specs=[pl.BlockSpec((1, tm, 2 * DIFF_QK), lambda b, i: (b, i, 0)),
                   pl.BlockSpec((1, DIFF_HEADS, LANES, tm), lambda b, i: (b, 0, 0, i)),
                   pl.BlockSpec((1, DIFF_HEADS, tm, DIFF_DV), lambda b, i: (b, 0, i, 0))],
        out_shape=[jax.ShapeDtypeStruct((bsz, t, 2 * DIFF_QK), BF16),
                   jax.ShapeDtypeStruct((bsz, DIFF_HEADS, LANES, t), BF16),
                   jax.ShapeDtypeStruct((bsz, DIFF_HEADS, t, DIFF_DV), BF16)],
        compiler_params=_params("parallel", "parallel"),
    )(proj, proj, proj, tile_g(g_q), tile_g(g_k), cos_t, sin_t)

    vec = lambda a: a.reshape(1, DIFF_DH).astype(F32)
    bound = (1.01 * DIFF_DH * (DIFF_DH ** -0.5 * LOG2E)
             * jnp.max(jnp.abs(g_q)) * jnp.max(jnp.abs(g_k))).astype(F32).reshape(1)
    return pl.pallas_call(
        functools.partial(_flash_kernel, tq=tq, tk=tk, nk=t // tk, lam_init=lam_init),
        grid=(bsz, DIFF_HEADS, t // tq),
        in_specs=[pl.BlockSpec(memory_space=pltpu.SMEM),
                  pl.BlockSpec((1, tq, 2 * LANES), lambda b, h, i: (b, i, h)),
                  pl.BlockSpec((1, 1, LANES, t), lambda b, h, i: (b, h, 0, 0)),
                  pl.BlockSpec((1, 1, t, DIFF_DV), lambda b, h, i: (b, h, 0, 0)),
                  pl.BlockSpec((1, DIFF_DH), lambda b, h, i: (0, 0)),
                  pl.BlockSpec((1, DIFF_DH), lambda b, h, i: (0, 0)),
                  pl.BlockSpec((1, DIFF_DH), lambda b, h, i: (0, 0)),
                  pl.BlockSpec((1, DIFF_DH), lambda b, h, i: (0, 0)),
                  pl.BlockSpec((1, DIFF_DV), lambda b, h, i: (0, 0))],
        out_specs=pl.BlockSpec((1, tq, DIFF_DV), lambda b, h, i: (b, i, h)),
        out_shape=jax.ShapeDtypeStruct((bsz, t, DIFF_V), BF16),
        scratch_shapes=[pltpu.VMEM((2 * tq, LANES), BF16),
                        pltpu.VMEM((2 * tq, LANES), F32),
                        pltpu.VMEM((2 * tq, LANES), F32),
                        pltpu.VMEM((2 * tq, DIFF_DV), F32)],
        compiler_params=_params("parallel", "parallel", "arbitrary"),
    )(bound, qz, kt, vh, vec(lq1), vec(lk1), vec(lq2), vec(lk2), g_sub.reshape(1, DIFF_DV).astype(F32))


def _mem_kernel(q_ref, kv_ref, gq_ref, gk_ref, o_ref):
    outs = []
    for h in range(MEM_HEADS):
        hs = slice(h * MEM_DH, (h + 1) * MEM_DH)
        q = q_ref[0, :, hs].astype(F32)
        q = q * lax.rsqrt(jnp.mean(q * q, axis=-1, keepdims=True) + RMS_EPS) * gq_ref[...]
        k = kv_ref[0, :, hs]
        k = k * lax.rsqrt(jnp.mean(k * k, axis=-1, keepdims=True) + RMS_EPS) * gk_ref[...]
        v = kv_ref[0, :, MEM_Q + h * MEM_DH:MEM_Q + (h + 1) * MEM_DH]
        s = _dot_nt(q.astype(BF16), k.astype(BF16)) * (MEM_DH ** -0.5)
        p = jnp.exp(s - jnp.max(s, axis=-1, keepdims=True))
        p = p / jnp.sum(p, axis=-1, keepdims=True)
        outs.append(_dot(p.astype(BF16), v.astype(BF16)))
    o_ref[0] = jnp.concatenate(outs, axis=1).astype(o_ref.dtype)


def _mem_branch(proj, kv, g_q, g_k, tm):
    bsz, t, _ = proj.shape
    tm = min(tm, t)
    return pl.pallas_call(
        _mem_kernel,
        grid=(bsz, t // tm),
        in_specs=[pl.BlockSpec((1, tm, MEM_Q), lambda b, i: (b, i, OFF_MQ // MEM_Q)),
                  pl.BlockSpec((1, N_MEM, 2 * MEM_Q), lambda b, i: (b, 0, 0)),
                  pl.BlockSpec((1, MEM_DH), lambda b, i: (0, 0)),
                  pl.BlockSpec((1, MEM_DH), lambda b, i: (0, 0))],
        out_specs=pl.BlockSpec((1, tm, MEM_Q), lambda b, i: (b, i, 0)),
        out_shape=jax.ShapeDtypeStruct((bsz, t, MEM_Q), BF16),
        compiler_params=_params("parallel", "parallel"),
    )(proj, kv, g_q.reshape(1, MEM_DH).astype(F32), g_k.reshape(1, MEM_DH).astype(F32))


def _merge_kernel(x_ref, pg_ref, pd_ref, pm_ref, g0_ref, g1_ref, g2_ref,
                  wg_ref, wd_ref, wm_ref, wo_ref, o_ref):
    merged = (jax.nn.sigmoid(g0_ref[...].astype(F32)) * _dot(pg_ref[...], wg_ref[...])
              + jax.nn.sigmoid(g1_ref[...].astype(F32)) * _dot(pd_ref[...], wd_ref[...])
              + jax.nn.sigmoid(g2_ref[...].astype(F32)) * _dot(pm_ref[...], wm_ref[...]))
    o_ref[...] = x_ref[...] + _dot(merged.astype(BF16), wo_ref[...])


def _merge(x, proj, pg, pd, pm, wg, wd, wm, wo, tm):
    n, d = x.shape
    tm = min(tm, n)
    row = lambda c: pl.BlockSpec((tm, d), lambda i: (i, c))
    full = pl.BlockSpec((d, d), lambda i: (0, 0))
    br = OFF_BR // d
    return pl.pallas_call(
        _merge_kernel,
        grid=(n // tm,),
        in_specs=[row(0), row(0), row(0), row(0), row(br), row(br + 1), row(br + 2), full, full, full, full],
        out_specs=row(0),
        out_shape=jax.ShapeDtypeStruct((n, d), F32),
        compiler_params=_params("parallel"),
    )(x, pg, pd, pm, proj, proj, proj, wg, wd, wm, wo)


PEER_ECHUNK = 2048
PEER_IPER = PEER_ECHUNK // PEER_NKEYS
PEER_PIECE = 2


def _bitonic_merge_desc(a):
    n = len(a)
    j = n // 2
    while j >= 1:
        for i in range(n):
            l = i ^ j
            if l > i:
                a[i], a[l] = jnp.maximum(a[i], a[l]), jnp.minimum(a[i], a[l])
        j //= 2
    return a


def _top16_desc(vals):
    a = list(vals)
    n = len(a)
    k = 2
    while k <= n:
        j = k // 2
        while j >= 1:
            for i in range(n):
                l = i ^ j
                if l > i:
                    hi, lo = jnp.maximum(a[i], a[l]), jnp.minimum(a[i], a[l])
                    a[i], a[l] = (hi, lo) if (i & k) == 0 else (lo, hi)
            j //= 2
        k *= 2
    for shift in (4, 2, 1):
        a = [jnp.maximum(a[r], pltpu.roll(a[n - 1 - r], shift, axis=0)) for r in range(n)]
        a = _bitonic_merge_desc(a)
    return a


def _dup_bf16_words(x):
    bits = lax.bitcast_convert_type(x.astype(BF16).astype(F32), jnp.uint32)
    return lax.bitcast_convert_type(bits | (bits >> 16), F32)


def _packed_row(row, nrows):
    return pltpu.bitcast(jnp.broadcast_to(row, (nrows // 2, row.shape[1])), BF16)


def _peer_kernel(x_ref, g_ref, wqt_ref, k1h_ref, k1l_ref, k2h_ref, k2l_ref, u_ref, vt_ref, o_ref,
                 hbt_ref, t_ref, c_ref, e2_ref, a_ref, wt_ref, acc_ref):
    ec = pl.program_id(1)
    tt = x_ref.shape[0]
    neg = jnp.full((SUBLANES, tt), -jnp.inf, F32)
    sub = lax.broadcasted_iota(jnp.int32, (SUBLANES, tt), 0)

    @pl.when(ec == 0)
    def _():
        x = x_ref[...]
        hb = x * lax.rsqrt(jnp.mean(x * x, axis=-1, keepdims=True) + RMS_EPS) * g_ref[...]
        hbt_ref[...] = hb.T.astype(BF16)
        acc_ref[...] = jnp.zeros_like(acc_ref)
        for h in range(PEER_HEADS):
            def scores(kh_ref, kl_ref, row0):
                qt = _dot(wqt_ref[row0:row0 + PEER_DQH, :], hbt_ref[...])
                qh = qt.astype(BF16)
                ql = (qt - qh.astype(F32)).astype(BF16)
                return _dot(kh_ref[...], qh) + _dot(kh_ref[...], ql) + _dot(kl_ref[...], qh)

            s1 = scores(k1h_ref, k1l_ref, h * PEER_DQ)
            s2 = scores(k2h_ref, k2l_ref, h * PEER_DQ + PEER_DQH)
            blocks = lambda s: [s[r * SUBLANES:(r + 1) * SUBLANES, :] for r in range(PEER_NKEYS // SUBLANES)]
            v1 = _top16_desc(blocks(s1))
            v2 = _top16_desc(blocks(s2))

            def pack(vs):
                out = neg
                for r in range(SUBLANES):
                    out = jnp.where(sub == r, vs[r], out)
                return out

            v2_lo, v2_hi, v1_hi = pack(v2[:SUBLANES]), pack(v2[SUBLANES:]), pack(v1[SUBLANES:])
            cand = [v1[0] + v2_lo, v1[0] + v2_hi, v1_hi + v2[0]]
            cand += [v1[a] + v2_lo for a in range(1, SUBLANES)]
            cand += [neg] * (PEER_TOPK - len(cand))
            top = _top16_desc(cand)
            tau16 = top[PEER_TOPK - 1]
            below = lambda s, bound: jnp.max(jnp.where(s < bound, s, -jnp.inf), axis=0, keepdims=True)
            tau17 = below(jnp.concatenate(cand[:10], axis=0), tau16[0:1, :])
            tau17 = jnp.maximum(tau17, below(s1, v1[PEER_TOPK - 1][0:1, :]) + v2[0][0:1, :])
            tau17 = jnp.maximum(tau17, below(s2, v2[PEER_TOPK - 1][0:1, :]) + v1[0][0:1, :])
            tau = 0.5 * (tau16[0:1, :] + tau17)
            z = top[0] - top[0]
            for r in range(PEER_TOPK):
                z = z + jnp.exp(top[r] - top[0])
            z = z[0:1, :]
            nchunks = PEER_NKEYS // PEER_IPER
            m2 = v2[0][0:1, :]
            t_ref[h] = _dup_bf16_words(jnp.exp(tau - s1 - m2)).reshape(nchunks, PEER_IPER, tt)
            c_ref[h] = _dup_bf16_words(0.5 * jnp.exp(s1 - v1[0][0:1, :]) / z).reshape(nchunks, PEER_IPER, tt)
            e2_ref[h] = jnp.exp(s2 - m2).astype(BF16)

    half_lanes = tt // 2

    def gates(il, ls):
        acc = None
        for h in range(PEER_HEADS):
            e2v = e2_ref[h, :, ls]
            thr = _packed_row(t_ref[h, ec, il:il + 1, ls], PEER_NKEYS)
            coef = _packed_row(c_ref[h, ec, il:il + 1, ls], PEER_NKEYS)
            term = jnp.where(e2v >= thr, e2v, jnp.zeros_like(e2v)) * coef
            acc = term if acc is None else acc + term
        return acc

    def weights(il, ls, g):
        rs = slice(il * PEER_NKEYS, (il + 1) * PEER_NKEYS)
        a = a_ref[rs, ls]
        gelu2 = a * (1.0 + lax.erf(a * (2.0 ** -0.5)))
        wt_ref[rs, ls] = gelu2.astype(BF16) * g

    def value_matmul(hs):
        acc_ref[:, hs] += _dot(vt_ref[0], wt_ref[:, hs])

    lane_blocks = [slice(lb * LANES, (lb + 1) * LANES) for lb in range(tt // LANES)]
    first, second = lane_blocks[:len(lane_blocks) // 2], lane_blocks[len(lane_blocks) // 2:]
    for lanes, hs in ((first, slice(0, half_lanes)), (second, slice(half_lanes, tt))):
        for piece in range(PEER_IPER // PEER_PIECE):
            if lanes is first:
                ps = slice(piece * PEER_PIECE * PEER_NKEYS, (piece + 1) * PEER_PIECE * PEER_NKEYS)
                a_ref[ps, :] = _dot(u_ref[ps, :], hbt_ref[...])
            for il in range(piece * PEER_PIECE, (piece + 1) * PEER_PIECE):
                for ls in lanes:
                    weights(il, ls, gates(il, ls))
        value_matmul(hs)

    @pl.when(ec == pl.num_programs(1) - 1)
    def _():
        o_ref[...] = x_ref[...] + acc_ref[...].T


def _peer(x, g_ffn, wqt, k1h, k1l, k2h, k2l, u, vt, tt):
    n, d = x.shape
    tt = min(tt, n)
    nec = PEER_EXPERTS // PEER_ECHUNK
    const = lambda shape: pl.BlockSpec(shape, lambda i, e: (0,) * len(shape))
    key_spec = const((PEER_NKEYS, PEER_DQH))
    assert tt % (2 * LANES) == 0 and n % tt == 0
    stat_i = pltpu.VMEM((PEER_HEADS, PEER_NKEYS // PEER_IPER, PEER_IPER, tt), F32)
    return pl.pallas_call(
        _peer_kernel,
        grid=(n // tt, nec),
        in_specs=[pl.BlockSpec((tt, d), lambda i, e: (i, 0)),
                  const((1, d)),
                  const((PEER_HEADS * PEER_DQ, d)),
                  key_spec, key_spec, key_spec, key_spec,
                  pl.BlockSpec((PEER_ECHUNK, d), lambda i, e: (e, 0)),
                  pl.BlockSpec((1, d, PEER_ECHUNK), lambda i, e: (e, 0, 0))],
        out_specs=pl.BlockSpec((tt, d), lambda i, e: (i, 0)),
        out_shape=jax.ShapeDtypeStruct((n, d), F32),
        scratch_shapes=[pltpu.VMEM((d, tt), BF16), stat_i, stat_i,
                        pltpu.VMEM((PEER_HEADS, PEER_NKEYS, tt), BF16),
                        pltpu.VMEM((PEER_ECHUNK, tt), F32),
                        pltpu.VMEM((PEER_ECHUNK, tt), BF16),
                        pltpu.VMEM((d, tt), F32)],
        compiler_params=_params("parallel", "arbitrary"),
    )(x, g_ffn.reshape(1, d), wqt, k1h, k1l, k2h, k2l, u, vt)


def _split_hi_lo(a):
    hi = a.astype(BF16)
    return hi, (a - hi.astype(F32)).astype(BF16)


def _prepare_weights(g_mix, w_in, gla_w_dec_f, gla_b_dec_f, gla_w_dec_b, gla_b_dec_b, gla_g_out, gla_w_o,
                     diff_g_q, diff_g_k, diff_lq1, diff_lk1, diff_lq2, diff_lk2, diff_g_sub, diff_w_o,
                     mem_g_norm, mem_w_kv, mem_g_q, mem_g_k, mem_w_o, w_out,
                     g_ffn, peer_w_q, peer_sub_k1, peer_sub_k2, peer_u, peer_v):
    w = w_in[0]
    lr0 = 2 * GLA_QK + 2 * GLA_V
    lr1 = lr0 + 2 * GLA_LOWRANK
    w_main = jnp.concatenate([w[:, :lr0], w[:, lr1:]], axis=1).astype(BF16)
    w_lr = jnp.pad(w[:, lr0:lr1], ((0, 0), (0, LANES - 2 * GLA_LOWRANK))).astype(BF16)
    k1h, k1l = _split_hi_lo(peer_sub_k1[0])
    k2h, k2l = _split_hi_lo(peer_sub_k2[0])
    return dict(
        g_mix=g_mix[0], w_main=w_main, w_lr=w_lr,
        gla=(gla_w_dec_f[0], gla_b_dec_f[0], gla_w_dec_b[0], gla_b_dec_b[0], gla_g_out[0]),
        diff=(diff_g_q[0], diff_g_k[0], diff_lq1[0], diff_lk1[0], diff_lq2[0], diff_lk2[0], diff_g_sub[0]),
        mem_g_norm=mem_g_norm[0], mem_w_kv=mem_w_kv[0].astype(BF16), mem_gq=mem_g_q[0], mem_gk=mem_g_k[0],
        w_o=(gla_w_o[0].astype(BF16), diff_w_o[0].astype(BF16), mem_w_o[0].astype(BF16), w_out[0].astype(BF16)),
        g_ffn=g_ffn[0], wqt=peer_w_q[0].T.astype(BF16), keys=(k1h, k1l, k2h, k2l),
        u=peer_u[0].astype(BF16),
        vt=peer_v[0].astype(BF16).reshape(PEER_EXPERTS // PEER_ECHUNK, PEER_ECHUNK, D_MODEL).transpose(0, 2, 1),
    )


def _trunk(x, mem, p):
    bsz, t, d = x.shape
    n = bsz * t
    lam_init = 0.8 - 0.6 * math.exp(-0.3 * 0)
    x2 = x.reshape(n, d)
    proj = _rms_matmul(x2, p["g_mix"], p["w_main"], BF16, 1024, 2048).reshape(bsz, t, MAIN_COLS)
    lr = _rms_matmul(x2, p["g_mix"], p["w_lr"], F32, 1024, LANES).reshape(bsz, t, LANES)
    kv = _rms_matmul(mem.reshape(bsz * N_MEM, d), p["mem_g_norm"], p["mem_w_kv"], F32, 512, 2 * MEM_Q)
    kv = kv.reshape(bsz, N_MEM, 2 * MEM_Q)
    pre_gla = _gla_branch(proj, lr, *p["gla"], tb=512)
    pre_diff = _diff_branch(proj, *p["diff"], lam_init=lam_init, tm=512, tq=512, tk=1024)
    pre_mem = _mem_branch(proj, kv, p["mem_gq"], p["mem_gk"], tm=512)
    x1 = _merge(x2, proj.reshape(n, MAIN_COLS), pre_gla.reshape(n, d), pre_diff.reshape(n, d),
                pre_mem.reshape(n, d), *p["w_o"], tm=512)
    y = _peer(x1, p["g_ffn"], p["wqt"], *p["keys"], p["u"], p["vt"], tt=512)
    return y.reshape(bsz, t, d)


def kernel(x_prompt, x_sample, mem_prompt, mem_sample, g_mix, w_in, gla_w_dec_f, gla_b_dec_f, gla_w_dec_b, gla_b_dec_b, gla_g_out, gla_w_o, diff_g_q, diff_g_k, diff_lq1, diff_lk1, diff_lq2, diff_lk2, diff_g_sub, diff_w_o, mem_g_norm, mem_w_kv, mem_g_q, mem_g_k, mem_w_o, w_out, g_ffn, peer_w_q, peer_sub_k1, peer_sub_k2, peer_u, peer_v):
    p = _prepare_weights(g_mix, w_in, gla_w_dec_f, gla_b_dec_f, gla_w_dec_b, gla_b_dec_b, gla_g_out, gla_w_o,
                         diff_g_q, diff_g_k, diff_lq1, diff_lk1, diff_lq2, diff_lk2, diff_g_sub, diff_w_o,
                         mem_g_norm, mem_w_kv, mem_g_q, mem_g_k, mem_w_o, w_out,
                         g_ffn, peer_w_q, peer_sub_k1, peer_sub_k2, peer_u, peer_v)
    return (_trunk(x_prompt, mem_prompt, p), _trunk(x_sample, mem_sample, p))
```

```python
import functools
import math

import jax
import jax.numpy as jnp
from jax import lax
from jax.experimental import pallas as pl
from jax.experimental.pallas import tpu as pltpu

F32 = jnp.float32
BF16 = jnp.bfloat16

D_MODEL = 1024
N_MEM = 256
RMS_EPS = 1e-6
ROPE_THETA = 10000.0

GLA_HEADS = 4
GLA_DK = 128
GLA_DV = 256
GLA_LOWRANK = 16
GLA_GATE_NORM = 16.0
GLA_QK = GLA_HEADS * GLA_DK
GLA_V = GLA_HEADS * GLA_DV

DIFF_HEADS = 8
DIFF_DH = 64
DIFF_DV = 2 * DIFF_DH
DIFF_QK = DIFF_HEADS * 2 * DIFF_DH
DIFF_V = DIFF_HEADS * DIFF_DV

MEM_HEADS = 4
MEM_DH = 256
MEM_Q = MEM_HEADS * MEM_DH

N_BRANCH = 3
PEER_HEADS = 8
PEER_NKEYS = 128
PEER_EXPERTS = PEER_NKEYS * PEER_NKEYS
PEER_DQ = 256
PEER_DQH = PEER_DQ // 2
PEER_TOPK = 16

LANES = 128
SUBLANES = 8
VMEM_LIMIT = 56 * 1024 * 1024

OFF_GQ = 0
OFF_GK = OFF_GQ + GLA_QK
OFF_GV = OFF_GK + GLA_QK
OFF_GG = OFF_GV + GLA_V
OFF_DQ = OFF_GG + GLA_V
OFF_DK = OFF_DQ + DIFF_QK
OFF_DV = OFF_DK + DIFF_QK
OFF_MQ = OFF_DV + DIFF_V
OFF_BR = OFF_MQ + MEM_Q
MAIN_COLS = OFF_BR + N_BRANCH * D_MODEL

LOG2E = 1.4426950408889634


def _dot(a, b):
    return jnp.dot(a, b, preferred_element_type=F32)


def _dot_nt(a, b):
    return lax.dot_general(a, b, (((1,), (1,)), ((), ())), preferred_element_type=F32)


def _params(*sem):
    return pltpu.CompilerParams(dimension_semantics=sem, vmem_limit_bytes=VMEM_LIMIT)


def _rms_matmul_kernel(x_ref, g_ref, w_ref, o_ref, h_ref):
    @pl.when(pl.program_id(1) == 0)
    def _():
        x = x_ref[...]
        ms = jnp.mean(x * x, axis=-1, keepdims=True)
        h_ref[...] = (x * lax.rsqrt(ms + RMS_EPS) * g_ref[...]).astype(BF16)

    o_ref[...] = _dot(h_ref[...], w_ref[...]).astype(o_ref.dtype)


def _rms_matmul(x, g, w, out_dtype, tm, tn):
    n, d = x.shape
    m = w.shape[1]
    tm = min(tm, n)
    tn = min(tn, m)
    return pl.pallas_call(
        _rms_matmul_kernel,
        grid=(n // tm, m // tn),
        in_specs=[pl.BlockSpec((tm, d), lambda i, j: (i, 0)),
                  pl.BlockSpec((1, d), lambda i, j: (0, 0)),
                  pl.BlockSpec((d, tn), lambda i, j: (0, j))],
        out_specs=pl.BlockSpec((tm, tn), lambda i, j: (i, j)),
        out_shape=jax.ShapeDtypeStruct((n, m), out_dtype),
        scratch_shapes=[pltpu.VMEM((tm, d), BF16)],
        compiler_params=_params("parallel", "arbitrary"),
    )(x, g.reshape(1, d), w)


GLA_CHUNK = 128


def _gla_kernel(*refs, reverse, nchunk, final):
    if final:
        (q_ref, k_ref, v_ref, lr_ref, wd_ref, bd_ref, of_ref, gate_ref, gout_ref, o_ref, st_ref) = refs
    else:
        (q_ref, k_ref, v_ref, lr_ref, wd_ref, bd_ref, o_ref, st_ref) = refs
    C = GLA_CHUNK

    @pl.when(pl.program_id(1) == 0)
    def _():
        st_ref[...] = jnp.zeros_like(st_ref)

    rows = lax.broadcasted_iota(jnp.int32, (C, C), 0)
    cols = lax.broadcasted_iota(jnp.int32, (C, C), 1)
    tri = jnp.where(cols <= rows, 1.0, 0.0).astype(BF16)
    keep = (cols >= rows) if reverse else (cols <= rows)
    scale = GLA_DK ** -0.5
    order = range(nchunk - 1, -1, -1) if reverse else range(nchunk)
    for ci in order:
        sl = pl.ds(ci * C, C)
        lr_c = lr_ref[0, sl, :].astype(BF16)
        for h in range(GLA_HEADS):
            dk = slice(h * GLA_DK, (h + 1) * GLA_DK)
            dv = slice(h * GLA_DV, (h + 1) * GLA_DV)
            q = q_ref[0, sl, dk].astype(F32) * scale
            k = k_ref[0, sl, dk].astype(F32)
            v = v_ref[0, sl, dv]
            pre = _dot(lr_c, wd_ref[h]) + bd_ref[h]
            la = (jnp.minimum(pre, 0.0) - jnp.log1p(jnp.exp(-jnp.abs(pre)))) * (1.0 / GLA_GATE_NORM)
            la_hi = la.astype(BF16)
            la_lo = (la - la_hi.astype(F32)).astype(BF16)
            b = _dot(tri, la_hi) + _dot(tri, la_lo)
            tot = b[C - 1:C, :]
            st = st_ref[h]
            vt = v.astype(F32).T.astype(BF16)
            if not reverse:
                q_in = (q * jnp.exp(b)).astype(BF16)
                q_st = q_in
                k_in = (k * jnp.exp(-b)).astype(BF16)
                k_st = (k * jnp.exp(tot - b)).astype(BF16)
            else:
                c = b - la
                q_in = (q * jnp.exp(-c)).astype(BF16)
                q_st = (q * jnp.exp(tot - c)).astype(BF16)
                k_in = (k * jnp.exp(c)).astype(BF16)
                k_st = k_in
            att = jnp.where(keep, _dot_nt(q_in, k_in), 0.0)
            o = _dot(att.astype(BF16), v) + _dot_nt(q_st, st.astype(BF16))
            st_ref[h] = jnp.exp(tot) * st + _dot(vt, k_st)
            if final:
                diag = jnp.sum(q * k, axis=-1, keepdims=True)
                o = of_ref[0, sl, dv] + o - diag * v.astype(F32)
                ms = jnp.mean(o * o, axis=-1, keepdims=True)
                o = o * lax.rsqrt(ms + RMS_EPS) * gout_ref[...]
                gt = gate_ref[0, sl, dv].astype(F32)
                o = o * (gt * jax.nn.sigmoid(gt))
            o_ref[0, sl, dv] = o.astype(o_ref.dtype)


def _gla_branch(proj, lr, w_dec_f, b_dec_f, w_dec_b, b_dec_b, g_out, tb):
    bsz, t, _ = proj.shape
    tb = min(tb, t)
    nblk = t // tb
    nchunk = tb // GLA_CHUNK

    def dec_weights(w_dec, b_dec, row0):
        w = jnp.zeros((GLA_HEADS, LANES, GLA_DK), F32)
        w = w.at[:, row0:row0 + GLA_LOWRANK, :].set(
            w_dec.reshape(GLA_LOWRANK, GLA_HEADS, GLA_DK).transpose(1, 0, 2))
        return w.astype(BF16), b_dec.reshape(GLA_HEADS, 1, GLA_DK).astype(F32)

    def call(reverse, o_fwd):
        wd, bd = dec_weights(w_dec_b, b_dec_b, GLA_LOWRANK) if reverse else dec_weights(w_dec_f, b_dec_f, 0)
        blk = (lambda j: nblk - 1 - j) if reverse else (lambda j: j)
        in_specs = [
            pl.BlockSpec((1, tb, GLA_QK), lambda b, j: (b, blk(j), OFF_GQ // GLA_QK)),
            pl.BlockSpec((1, tb, GLA_QK), lambda b, j: (b, blk(j), OFF_GK // GLA_QK)),
            pl.BlockSpec((1, tb, GLA_V), lambda b, j: (b, blk(j), OFF_GV // GLA_V)),
            pl.BlockSpec((1, tb, LANES), lambda b, j: (b, blk(j), 0)),
            pl.BlockSpec((GLA_HEADS, LANES, GLA_DK), lambda b, j: (0, 0, 0)),
            pl.BlockSpec((GLA_HEADS, 1, GLA_DK), lambda b, j: (0, 0, 0)),
        ]
        args = [proj, proj, proj, lr, wd, bd]
        if reverse:
            in_specs += [
                pl.BlockSpec((1, tb, GLA_V), lambda b, j: (b, blk(j), 0)),
                pl.BlockSpec((1, tb, GLA_V), lambda b, j: (b, blk(j), OFF_GG // GLA_V)),
                pl.BlockSpec((1, GLA_DV), lambda b, j: (0, 0)),
            ]
            args += [o_fwd, proj, g_out.reshape(1, GLA_DV).astype(F32)]
        return pl.pallas_call(
            functools.partial(_gla_kernel, reverse=reverse, nchunk=nchunk, final=reverse),
            grid=(bsz, nblk),
            in_specs=in_specs,
            out_specs=pl.BlockSpec((1, tb, GLA_V), lambda b, j: (b, blk(j), 0)),
            out_shape=jax.ShapeDtypeStruct((bsz, t, GLA_V), BF16 if reverse else F32),
            scratch_shapes=[pltpu.VMEM((GLA_HEADS, GLA_DV, GLA_DK), F32)],
            compiler_params=_params("parallel", "arbitrary"),
        )(*args)

    return call(True, call(False, None))


def _group_sumsq(x, ones_bd):
    x2 = x * x
    hi = x2.astype(BF16)
    lo = (x2 - hi.astype(F32)).astype(BF16)
    pieces = []
    for c in range(x.shape[1] // LANES):
        cs = slice(c * LANES, (c + 1) * LANES)
        pieces.append(_dot(hi[:, cs], ones_bd) + _dot(lo[:, cs], ones_bd))
    return jnp.concatenate(pieces, axis=1)


def _diff_prep_kernel(q_ref, k_ref, v_ref, gq_ref, gk_ref, cos_ref, sin_ref, qz_ref, kt_ref, vh_ref):
    tm = q_ref.shape[1]
    r = lax.broadcasted_iota(jnp.int32, (LANES, LANES), 0) // DIFF_DH
    c = lax.broadcasted_iota(jnp.int32, (LANES, LANES), 1) // DIFF_DH
    ones_bd = jnp.where(r == c, 1.0, 0.0).astype(BF16)
    lane = lax.broadcasted_iota(jnp.int32, (tm, DIFF_QK), 1)
    first_half = (lane % DIFF_DH) < (DIFF_DH // 2)
    reps = DIFF_QK // LANES
    cos = jnp.concatenate([cos_ref[...]] * reps, axis=1)
    sin = jnp.concatenate([sin_ref[...]] * reps, axis=1)

    def norm_rope(x, g):
        ms = _group_sumsq(x, ones_bd) * (1.0 / DIFF_DH)
        xn = x * lax.rsqrt(ms + RMS_EPS) * g
        partner = jnp.where(first_half,
                            pltpu.roll(xn, DIFF_QK - DIFF_DH // 2, axis=1),
                            pltpu.roll(xn, DIFF_DH // 2, axis=1))
        return xn * cos + partner * sin

    qr = norm_rope(q_ref[0].astype(F32), gq_ref[...]) * (DIFF_DH ** -0.5 * LOG2E)
    kr = norm_rope(k_ref[0].astype(F32), gk_ref[...])
    lane_h = lax.broadcasted_iota(jnp.int32, (tm, LANES), 1)
    pieces = []
    for h in range(DIFF_HEADS):
        qh = qr[:, h * LANES:(h + 1) * LANES]
        pieces.append(jnp.where(lane_h < DIFF_DH, qh, 0.0))
        pieces.append(jnp.where(lane_h >= DIFF_DH, qh, 0.0))
        kt_ref[0, h] = kr[:, h * LANES:(h + 1) * LANES].T.astype(BF16)
        vh_ref[0, h] = v_ref[0, :, h * DIFF_DV:(h + 1) * DIFF_DV]
    qz_ref[0] = jnp.concatenate(pieces, axis=1).astype(BF16)


FLASH_SAFE_LOG2_RANGE = 60.0


def _flash_kernel(bound_ref, qz_ref, kt_ref, v_ref, lq1_ref, lk1_ref, lq2_ref, lk2_ref, gsub_ref, o_ref,
                  qs_ref, m_ref, l_ref, acc_ref, *, tq, tk, nk, lam_init):
    qs_ref[0:tq, :] = qz_ref[0, :, 0:LANES]
    qs_ref[tq:2 * tq, :] = qz_ref[0, :, LANES:2 * LANES]
    reps = tk // LANES
    bound = bound_ref[0]

    def scores(i):
        k0 = pl.multiple_of(i * tk, tk)
        s = _dot(qs_ref[...], kt_ref[0, 0, :, pl.ds(k0, tk)])
        return s, v_ref[0, 0, pl.ds(k0, tk), :]

    def finalize(o):
        lam = (jnp.exp(jnp.sum(lq1_ref[...] * lk1_ref[...], axis=1, keepdims=True))
               - jnp.exp(jnp.sum(lq2_ref[...] * lk2_ref[...], axis=1, keepdims=True)) + lam_init)
        o = o[0:tq] - lam * o[tq:2 * tq]
        ms = jnp.mean(o * o, axis=-1, keepdims=True)
        o = o * lax.rsqrt(ms + RMS_EPS) * gsub_ref[...] * (1.0 - lam_init)
        o_ref[0] = o.astype(o_ref.dtype)

    @pl.when(bound <= FLASH_SAFE_LOG2_RANGE)
    def _():
        l_ref[...] = jnp.zeros_like(l_ref)
        acc_ref[...] = jnp.zeros_like(acc_ref)

        def step(i, carry):
            s, v = scores(i)
            p = jnp.exp2(s - bound)
            part = p[:, 0:LANES]
            for r in range(1, reps):
                part = part + p[:, r * LANES:(r + 1) * LANES]
            l_ref[...] += part
            acc_ref[...] += _dot(p.astype(BF16), v)
            return carry

        lax.fori_loop(0, nk, step, 0)
        finalize(acc_ref[...] / jnp.sum(l_ref[...], axis=1, keepdims=True))

    @pl.when(bound > FLASH_SAFE_LOG2_RANGE)
    def _():
        m_ref[...] = jnp.full_like(m_ref, -jnp.inf)
        l_ref[...] = jnp.zeros_like(l_ref)
        acc_ref[...] = jnp.zeros_like(acc_ref)

        def step(i, carry):
            s, v = scores(i)
            m_prev = m_ref[...]
            m_new = jnp.maximum(m_prev, jnp.max(s, axis=1, keepdims=True))
            alpha = jnp.exp2(m_prev - m_new)
            p = jnp.exp2(s - jnp.concatenate([m_new] * reps, axis=1))
            l_ref[...] = alpha * l_ref[...] + jnp.sum(p, axis=1, keepdims=True)
            acc_ref[...] = alpha * acc_ref[...] + _dot(p.astype(BF16), v)
            m_ref[...] = m_new
            return carry

        lax.fori_loop(0, nk, step, 0)
        finalize(acc_ref[...] / l_ref[...])


def _rope_tables(t):
    half = DIFF_DH // 2
    inv_freq = ROPE_THETA ** (-jnp.arange(0, DIFF_DH, 2, dtype=F32) / DIFF_DH)
    ang = jnp.arange(t, dtype=F32)[:, None] * inv_freq[None, :]
    cos, sin = jnp.cos(ang), jnp.sin(ang)
    cos_t = jnp.tile(cos, (1, LANES // half))
    sin_t = jnp.tile(jnp.concatenate([-sin, sin], axis=1), (1, LANES // DIFF_DH))
    return cos_t, sin_t


def _diff_branch(proj, g_q, g_k, lq1, lk1, lq2, lk2, g_sub, lam_init, tm, tq, tk):
    bsz, t, _ = proj.shape
    tm, tq, tk = min(tm, t), min(tq, t), min(tk, t)
    cos_t, sin_t = _rope_tables(t)
    tile_g = lambda g: jnp.tile(g.astype(F32), DIFF_QK // DIFF_DH).reshape(1, DIFF_QK)
    qz, kt, vh = pl.pallas_call(
        _diff_prep_kernel,
        grid=(bsz, t // tm),
        in_specs=[pl.BlockSpec((1, tm, DIFF_QK), lambda b, i: (b, i, OFF_DQ // DIFF_QK)),
                  pl.BlockSpec((1, tm, DIFF_QK), lambda b, i: (b, i, OFF_DK // DIFF_QK)),
                  pl.BlockSpec((1, tm, DIFF_V), lambda b, i: (b, i, OFF_DV // DIFF_V)),
                  pl.BlockSpec((1, DIFF_QK), lambda b, i: (0, 0)),
                  pl.BlockSpec((1, DIFF_QK), lambda b, i: (0, 0)),
                  pl.BlockSpec((tm, LANES), lambda b, i: (i, 0)),
                  pl.BlockSpec((tm, LANES), lambda b, i: (i, 0))],
        out_specs=[pl.BlockSpec((1, tm, 2 * DIFF_QK), lambda b, i: (b, i, 0)),
                   pl.BlockSpec((1, DIFF_HEADS, LANES, tm), lambda b, i: (b, 0, 0, i)),
                   pl.BlockSpec((1, DIFF_HEADS, tm, DIFF_DV), lambda b, i: (b, 0, i, 0))],
        out_shape=[jax.ShapeDtypeStruct((bsz, t, 2 * DIFF_QK), BF16),
                   jax.ShapeDtypeStruct((bsz, DIFF_HEADS, LANES, t), BF16),
                   jax.ShapeDtypeStruct((bsz, DIFF_HEADS, t, DIFF_DV), BF16)],
        compiler_params=_params("parallel", "parallel"),
    )(proj, proj, proj, tile_g(g_q), tile_g(g_k), cos_t, sin_t)

    vec = lambda a: a.reshape(1, DIFF_DH).astype(F32)
    bound = (1.01 * DIFF_DH * (DIFF_DH ** -0.5 * LOG2E)
             * jnp.max(jnp.abs(g_q)) * jnp.max(jnp.abs(g_k))).astype(F32).reshape(1)
    return pl.pallas_call(
        functools.partial(_flash_kernel, tq=tq, tk=tk, nk=t // tk, lam_init=lam_init),
        grid=(bsz, DIFF_HEADS, t // tq),
        in_specs=[pl.BlockSpec(memory_space=pltpu.SMEM),
                  pl.BlockSpec((1, tq, 2 * LANES), lambda b, h, i: (b, i, h)),
                  pl.BlockSpec((1, 1, LANES, t), lambda b, h, i: (b, h, 0, 0)),
                  pl.BlockSpec((1, 1, t, DIFF_DV), lambda b, h, i: (b, h, 0, 0)),
                  pl.BlockSpec((1, DIFF_DH), lambda b, h, i: (0, 0)),
                  pl.BlockSpec((1, DIFF_DH), lambda b, h, i: (0, 0)),
                  pl.BlockSpec((1, DIFF_DH), lambda b, h, i: (0, 0)),
                  pl.BlockSpec((1, DIFF_DH), lambda b, h, i: (0, 0)),
                  pl.BlockSpec((1, DIFF_DV), lambda b, h, i: (0, 0))],
        out_specs=pl.BlockSpec((1, tq, DIFF_DV), lambda b, h, i: (b, i, h)),
        out_shape=jax.ShapeDtypeStruct((bsz, t, DIFF_V), BF16),
        scratch_shapes=[pltpu.VMEM((2 * tq, LANES), BF16),
                        pltpu.VMEM((2 * tq, LANES), F32),
                        pltpu.VMEM((2 * tq, LANES), F32),
                        pltpu.VMEM((2 * tq, DIFF_DV), F32)],
        compiler_params=_params("parallel", "parallel", "arbitrary"),
    )(bound, qz, kt, vh, vec(lq1), vec(lk1), vec(lq2), vec(lk2), g_sub.reshape(1, DIFF_DV).astype(F32))


def _mem_kernel(q_ref, kv_ref, gq_ref, gk_ref, o_ref):
    outs = []
    for h in range(MEM_HEADS):
        hs = slice(h * MEM_DH, (h + 1) * MEM_DH)
        q = q_ref[0, :, hs].astype(F32)
        q = q * lax.rsqrt(jnp.mean(q * q, axis=-1, keepdims=True) + RMS_EPS) * gq_ref[...]
        k = kv_ref[0, :, hs]
        k = k * lax.rsqrt(jnp.mean(k * k, axis=-1, keepdims=True) + RMS_EPS) * gk_ref[...]
        v = kv_ref[0, :, MEM_Q + h * MEM_DH:MEM_Q + (h + 1) * MEM_DH]
        s = _dot_nt(q.astype(BF16), k.astype(BF16)) * (MEM_DH ** -0.5)
        p = jnp.exp(s - jnp.max(s, axis=-1, keepdims=True))
        p = p / jnp.sum(p, axis=-1, keepdims=True)
        outs.append(_dot(p.astype(BF16), v.astype(BF16)))
    o_ref[0] = jnp.concatenate(outs, axis=1).astype(o_ref.dtype)


def _mem_branch(proj, kv, g_q, g_k, tm):
    bsz, t, _ = proj.shape
    tm = min(tm, t)
    return pl.pallas_call(
        _mem_kernel,
        grid=(bsz, t // tm),
        in_specs=[pl.BlockSpec((1, tm, MEM_Q), lambda b, i: (b, i, OFF_MQ // MEM_Q)),
                  pl.BlockSpec((1, N_MEM, 2 * MEM_Q), lambda b, i: (b, 0, 0)),
                  pl.BlockSpec((1, MEM_DH), lambda b, i: (0, 0)),
                  pl.BlockSpec((1, MEM_DH), lambda b, i: (0, 0))],
        out_specs=pl.BlockSpec((1, tm, MEM_Q), lambda b, i: (b, i, 0)),
        out_shape=jax.ShapeDtypeStruct((bsz, t, MEM_Q), BF16),
        compiler_params=_params("parallel", "parallel"),
    )(proj, kv, g_q.reshape(1, MEM_DH).astype(F32), g_k.reshape(1, MEM_DH).astype(F32))


def _merge_kernel(x_ref, pg_ref, pd_ref, pm_ref, g0_ref, g1_ref, g2_ref,
                  wg_ref, wd_ref, wm_ref, wo_ref, o_ref):
    merged = (jax.nn.sigmoid(g0_ref[...].astype(F32)) * _dot(pg_ref[...], wg_ref[...])
              + jax.nn.sigmoid(g1_ref[...].astype(F32)) * _dot(pd_ref[...], wd_ref[...])
              + jax.nn.sigmoid(g2_ref[...].astype(F32)) * _dot(pm_ref[...], wm_ref[...]))
    o_ref[...] = x_ref[...] + _dot(merged.astype(BF16), wo_ref[...])


def _merge(x, proj, pg, pd, pm, wg, wd, wm, wo, tm):
    n, d = x.shape
    tm = min(tm, n)
    row = lambda c: pl.BlockSpec((tm, d), lambda i: (i, c))
    full = pl.BlockSpec((d, d), lambda i: (0, 0))
    br = OFF_BR // d
    return pl.pallas_call(
        _merge_kernel,
        grid=(n // tm,),
        in_specs=[row(0), row(0), row(0), row(0), row(br), row(br + 1), row(br + 2), full, full, full, full],
        out_specs=row(0),
        out_shape=jax.ShapeDtypeStruct((n, d), F32),
        compiler_params=_params("parallel"),
    )(x, pg, pd, pm, proj, proj, proj, wg, wd, wm, wo)


PEER_ECHUNK = 2048
PEER_IPER = PEER_ECHUNK // PEER_NKEYS


def _bitonic_merge_desc(a):
    n = len(a)
    j = n // 2
    while j >= 1:
        for i in range(n):
            l = i ^ j
            if l > i:
                a[i], a[l] = jnp.maximum(a[i], a[l]), jnp.minimum(a[i], a[l])
        j //= 2
    return a


def _top16_desc(vals):
    a = list(vals)
    n = len(a)
    k = 2
    while k <= n:
        j = k // 2
        while j >= 1:
            for i in range(n):
                l = i ^ j
                if l > i:
                    hi, lo = jnp.maximum(a[i], a[l]), jnp.minimum(a[i], a[l])
                    a[i], a[l] = (hi, lo) if (i & k) == 0 else (lo, hi)
            j //= 2
        k *= 2
    for shift in (4, 2, 1):
        a = [jnp.maximum(a[r], pltpu.roll(a[n - 1 - r], shift, axis=0)) for r in range(n)]
        a = _bitonic_merge_desc(a)
    return a


def _dup_bf16_words(x):
    bits = lax.bitcast_convert_type(x.astype(BF16).astype(F32), jnp.uint32)
    return lax.bitcast_convert_type(bits | (bits >> 16), F32)


def _packed_row(row, nrows):
    return pltpu.bitcast(jnp.broadcast_to(row, (nrows // 2, row.shape[1])), BF16)


def _peer_kernel(x_ref, g_ref, wqt_ref, k1h_ref, k1l_ref, k2h_ref, k2l_ref, u_ref, vt_ref, o_ref,
                 hbt_ref, t_ref, c_ref, e2_ref, a_ref, wt_ref, acc_ref):
    ec = pl.program_id(1)
    tt = x_ref.shape[0]
    neg = jnp.full((SUBLANES, tt), -jnp.inf, F32)
    sub = lax.broadcasted_iota(jnp.int32, (SUBLANES, tt), 0)

    @pl.when(ec == 0)
    def _():
        x = x_ref[...]
        hb = x * lax.rsqrt(jnp.mean(x * x, axis=-1, keepdims=True) + RMS_EPS) * g_ref[...]
        hbt_ref[...] = hb.T.astype(BF16)
        acc_ref[...] = jnp.zeros_like(acc_ref)
        for h in range(PEER_HEADS):
            def scores(kh_ref, kl_ref, row0):
                qt = _dot(wqt_ref[row0:row0 + PEER_DQH, :], hbt_ref[...])
                qh = qt.astype(BF16)
                ql = (qt - qh.astype(F32)).astype(BF16)
                return _dot(kh_ref[...], qh) + _dot(kh_ref[...], ql) + _dot(kl_ref[...], qh)

            s1 = scores(k1h_ref, k1l_ref, h * PEER_DQ)
            s2 = scores(k2h_ref, k2l_ref, h * PEER_DQ + PEER_DQH)
            blocks = lambda s: [s[r * SUBLANES:(r + 1) * SUBLANES, :] for r in range(PEER_NKEYS // SUBLANES)]
            v1 = _top16_desc(blocks(s1))
            v2 = _top16_desc(blocks(s2))

            def pack(vs):
                out = neg
                for r in range(SUBLANES):
                    out = jnp.where(sub == r, vs[r], out)
                return out

            v2_lo, v2_hi, v1_hi = pack(v2[:SUBLANES]), pack(v2[SUBLANES:]), pack(v1[SUBLANES:])
            cand = [v1[0] + v2_lo, v1[0] + v2_hi, v1_hi + v2[0]]
            cand += [v1[a] + v2_lo for a in range(1, SUBLANES)]
            cand += [neg] * (PEER_TOPK - len(cand))
            top = _top16_desc(cand)
            tau16 = top[PEER_TOPK - 1]
            below = lambda s, bound: jnp.max(jnp.where(s < bound, s, -jnp.inf), axis=0, keepdims=True)
            tau17 = below(jnp.concatenate(cand[:10], axis=0), tau16[0:1, :])
            tau17 = jnp.maximum(tau17, below(s1, v1[PEER_TOPK - 1][0:1, :]) + v2[0][0:1, :])
            tau17 = jnp.maximum(tau17, below(s2, v2[PEER_TOPK - 1][0:1, :]) + v1[0][0:1, :])
            tau = 0.5 * (tau16[0:1, :] + tau17)
            z = top[0] - top[0]
            for r in range(PEER_TOPK):
                z = z + jnp.exp(top[r] - top[0])
            z = z[0:1, :]
            nchunks = PEER_NKEYS // PEER_IPER
            m2 = v2[0][0:1, :]
            t_ref[h] = _dup_bf16_words(jnp.exp(tau - s1 - m2)).reshape(nchunks, PEER_IPER, tt)
            c_ref[h] = _dup_bf16_words(0.5 * jnp.exp(s1 - v1[0][0:1, :]) / z).reshape(nchunks, PEER_IPER, tt)
            e2_ref[h] = jnp.exp(s2 - m2).astype(BF16)

    half_lanes = tt // 2

    def gates(il, ls):
        acc = None
        for h in range(PEER_HEADS):
            e2v = e2_ref[h, :, ls]
            thr = _packed_row(t_ref[h, ec, il:il + 1, ls], PEER_NKEYS)
            coef = _packed_row(c_ref[h, ec, il:il + 1, ls], PEER_NKEYS)
            term = jnp.where(e2v >= thr, e2v, jnp.zeros_like(e2v)) * coef
            acc = term if acc is None else acc + term
        return acc

    def weights(il, ls, g):
        rs = slice(il * PEER_NKEYS, (il + 1) * PEER_NKEYS)
        a = a_ref[rs, ls]
        gelu2 = a * (1.0 + lax.erf(a * (2.0 ** -0.5)))
        wt_ref[rs, ls] = gelu2.astype(BF16) * g

    def value_matmul(hs):
        acc_ref[:, hs] += _dot(vt_ref[0], wt_ref[:, hs])

    a_ref[...] = _dot(u_ref[...], hbt_ref[...])
    lane_blocks = [slice(lb * LANES, (lb + 1) * LANES) for lb in range(tt // LANES)]
    first, second = lane_blocks[:len(lane_blocks) // 2], lane_blocks[len(lane_blocks) // 2:]
    for lanes, hs in ((first, slice(0, half_lanes)), (second, slice(half_lanes, tt))):
        for il in range(PEER_IPER):
            for ls in lanes:
                weights(il, ls, gates(il, ls))
        value_matmul(hs)

    @pl.when(ec == pl.num_programs(1) - 1)
    def _():
        o_ref[...] = x_ref[...] + acc_ref[...].T


def _peer(x, g_ffn, wqt, k1h, k1l, k2h, k2l, u, vt, tt):
    n, d = x.shape
    tt = min(tt, n)
    nec = PEER_EXPERTS // PEER_ECHUNK
    const = lambda shape: pl.BlockSpec(shape, lambda i, e: (0,) * len(shape))
    key_spec = const((PEER_NKEYS, PEER_DQH))
    assert tt % (2 * LANES) == 0 and n % tt == 0
    stat_i = pltpu.VMEM((PEER_HEADS, PEER_NKEYS // PEER_IPER, PEER_IPER, tt), F32)
    return pl.pallas_call(
        _peer_kernel,
        grid=(n // tt, nec),
        in_specs=[pl.BlockSpec((tt, d), lambda i, e: (i, 0)),
                  const((1, d)),
                  const((PEER_HEADS * PEER_DQ, d)),
                  key_spec, key_spec, key_spec, key_spec,
                  pl.BlockSpec((PEER_ECHUNK, d), lambda i, e: (e, 0)),
                  pl.BlockSpec((1, d, PEER_ECHUNK), lambda i, e: (e, 0, 0))],
        out_specs=pl.BlockSpec((tt, d), lambda i, e: (i, 0)),
        out_shape=jax.ShapeDtypeStruct((n, d), F32),
        scratch_shapes=[pltpu.VMEM((d, tt), BF16), stat_i, stat_i,
                        pltpu.VMEM((PEER_HEADS, PEER_NKEYS, tt), BF16),
                        pltpu.VMEM((PEER_ECHUNK, tt), F32),
                        pltpu.VMEM((PEER_ECHUNK, tt), BF16),
                        pltpu.VMEM((d, tt), F32)],
        compiler_params=_params("parallel", "arbitrary"),
    )(x, g_ffn.reshape(1, d), wqt, k1h, k1l, k2h, k2l, u, vt)


def _split_hi_lo(a):
    hi = a.astype(BF16)
    return hi, (a - hi.astype(F32)).astype(BF16)


def _prepare_weights(g_mix, w_in, gla_w_dec_f, gla_b_dec_f, gla_w_dec_b, gla_b_dec_b, gla_g_out, gla_w_o,
                     diff_g_q, diff_g_k, diff_lq1, diff_lk1, diff_lq2, diff_lk2, diff_g_sub, diff_w_o,
                     mem_g_norm, mem_w_kv, mem_g_q, mem_g_k, mem_w_o, w_out,
                     g_ffn, peer_w_q, peer_sub_k1, peer_sub_k2, peer_u, peer_v):
    w = w_in[0]
    lr0 = 2 * GLA_QK + 2 * GLA_V
    lr1 = lr0 + 2 * GLA_LOWRANK
    w_main = jnp.concatenate([w[:, :lr0], w[:, lr1:]], axis=1).astype(BF16)
    w_lr = jnp.pad(w[:, lr0:lr1], ((0, 0), (0, LANES - 2 * GLA_LOWRANK))).astype(BF16)
    k1h, k1l = _split_hi_lo(peer_sub_k1[0])
    k2h, k2l = _split_hi_lo(peer_sub_k2[0])
    return dict(
        g_mix=g_mix[0], w_main=w_main, w_lr=w_lr,
        gla=(gla_w_dec_f[0], gla_b_dec_f[0], gla_w_dec_b[0], gla_b_dec_b[0], gla_g_out[0]),
        diff=(diff_g_q[0], diff_g_k[0], diff_lq1[0], diff_lk1[0], diff_lq2[0], diff_lk2[0], diff_g_sub[0]),
        mem_g_norm=mem_g_norm[0], mem_w_kv=mem_w_kv[0].astype(BF16), mem_gq=mem_g_q[0], mem_gk=mem_g_k[0],
        w_o=(gla_w_o[0].astype(BF16), diff_w_o[0].astype(BF16), mem_w_o[0].astype(BF16), w_out[0].astype(BF16)),
        g_ffn=g_ffn[0], wqt=peer_w_q[0].T.astype(BF16), keys=(k1h, k1l, k2h, k2l),
        u=peer_u[0].astype(BF16),
        vt=peer_v[0].astype(BF16).reshape(PEER_EXPERTS // PEER_ECHUNK, PEER_ECHUNK, D_MODEL).transpose(0, 2, 1),
    )


def _trunk(x, mem, p):
    bsz, t, d = x.shape
    n = bsz * t
    lam_init = 0.8 - 0.6 * math.exp(-0.3 * 0)
    x2 = x.reshape(n, d)
    proj = _rms_matmul(x2, p["g_mix"], p["w_main"], BF16, 1024, 2048).reshape(bsz, t, MAIN_COLS)
    lr = _rms_matmul(x2, p["g_mix"], p["w_lr"], F32, 1024, LANES).reshape(bsz, t, LANES)
    kv = _rms_matmul(mem.reshape(bsz * N_MEM, d), p["mem_g_norm"], p["mem_w_kv"], F32, 512, 2 * MEM_Q)
    kv = kv.reshape(bsz, N_MEM, 2 * MEM_Q)
    pre_gla = _gla_branch(proj, lr, *p["gla"], tb=512)
    pre_diff = _diff_branch(proj, *p["diff"], lam_init=lam_init, tm=512, tq=512, tk=1024)
    pre_mem = _mem_branch(proj, kv, p["mem_gq"], p["mem_gk"], tm=512)
    x1 = _merge(x2, proj.reshape(n, MAIN_COLS), pre_gla.reshape(n, d), pre_diff.reshape(n, d),
                pre_mem.reshape(n, d), *p["w_o"], tm=512)
    y = _peer(x1, p["g_ffn"], p["wqt"], *p["keys"], p["u"], p["vt"], tt=512)
    return y.reshape(bsz, t, d)


def kernel(x_prompt, x_sample, mem_prompt, mem_sample, g_mix, w_in, gla_w_dec_f, gla_b_dec_f, gla_w_dec_b, gla_b_dec_b, gla_g_out, gla_w_o, diff_g_q, diff_g_k, diff_lq1, diff_lk1, diff_lq2, diff_lk2, diff_g_sub, diff_w_o, mem_g_norm, mem_w_kv, mem_g_q, mem_g_k, mem_w_o, w_out, g_ffn, peer_w_q, peer_sub_k1, peer_sub_k2, peer_u, peer_v):
    p = _prepare_weights(g_mix, w_in, gla_w_dec_f, gla_b_dec_f, gla_w_dec_b, gla_b_dec_b, gla_g_out, gla_w_o,
                         diff_g_q, diff_g_k, diff_lq1, diff_lk1, diff_lq2, diff_lk2, diff_g_sub, diff_w_o,
                         mem_g_norm, mem_w_kv, mem_g_q, mem_g_k, mem_w_o, w_out,
                         g_ffn, peer_w_q, peer_sub_k1, peer_sub_k2, peer_u, peer_v)
    return (_trunk(x_prompt, mem_prompt, p), _trunk(x_sample, mem_sample, p))
```

```python
import functools
import math

import jax
import jax.numpy as jnp
from jax import lax
from jax.experimental import pallas as pl
from jax.experimental.pallas import tpu as pltpu

F32 = jnp.float32
BF16 = jnp.bfloat16

D_MODEL = 1024
N_MEM = 256
RMS_EPS = 1e-6
ROPE_THETA = 10000.0

GLA_HEADS = 4
GLA_DK = 128
GLA_DV = 256
GLA_LOWRANK = 16
GLA_GATE_NORM = 16.0
GLA_QK = GLA_HEADS * GLA_DK
GLA_V = GLA_HEADS * GLA_DV

DIFF_HEADS = 8
DIFF_DH = 64
DIFF_DV = 2 * DIFF_DH
DIFF_QK = DIFF_HEADS * 2 * DIFF_DH
DIFF_V = DIFF_HEADS * DIFF_DV

MEM_HEADS = 4
MEM_DH = 256
MEM_Q = MEM_HEADS * MEM_DH

N_BRANCH = 3
PEER_HEADS = 8
PEER_NKEYS = 128
PEER_EXPERTS = PEER_NKEYS * PEER_NKEYS
PEER_DQ = 256
PEER_DQH = PEER_DQ // 2
PEER_TOPK = 16

LANES = 128
SUBLANES = 8
VMEM_LIMIT = 56 * 1024 * 1024

OFF_GQ = 0
OFF_GK = OFF_GQ + GLA_QK
OFF_GV = OFF_GK + GLA_QK
OFF_GG = OFF_GV + GLA_V
OFF_DQ = OFF_GG + GLA_V
OFF_DK = OFF_DQ + DIFF_QK
OFF_DV = OFF_DK + DIFF_QK
OFF_MQ = OFF_DV + DIFF_V
OFF_BR = OFF_MQ + MEM_Q
MAIN_COLS = OFF_BR + N_BRANCH * D_MODEL

LOG2E = 1.4426950408889634


def _dot(a, b):
    return jnp.dot(a, b, preferred_element_type=F32)


def _dot_nt(a, b):
    return lax.dot_general(a, b, (((1,), (1,)), ((), ())), preferred_element_type=F32)


def _params(*sem):
    return pltpu.CompilerParams(dimension_semantics=sem, vmem_limit_bytes=VMEM_LIMIT)


def _rms_matmul_kernel(x_ref, g_ref, w_ref, o_ref, h_ref):
    @pl.when(pl.program_id(1) == 0)
    def _():
        x = x_ref[...]
        ms = jnp.mean(x * x, axis=-1, keepdims=True)
        h_ref[...] = (x * lax.rsqrt(ms + RMS_EPS) * g_ref[...]).astype(BF16)

    o_ref[...] = _dot(h_ref[...], w_ref[...]).astype(o_ref.dtype)


def _rms_matmul(x, g, w, out_dtype, tm, tn):
    n, d = x.shape
    m = w.shape[1]
    tm = min(tm, n)
    tn = min(tn, m)
    return pl.pallas_call(
        _rms_matmul_kernel,
        grid=(n // tm, m // tn),
        in_specs=[pl.BlockSpec((tm, d), lambda i, j: (i, 0)),
                  pl.BlockSpec((1, d), lambda i, j: (0, 0)),
                  pl.BlockSpec((d, tn), lambda i, j: (0, j))],
        out_specs=pl.BlockSpec((tm, tn), lambda i, j: (i, j)),
        out_shape=jax.ShapeDtypeStruct((n, m), out_dtype),
        scratch_shapes=[pltpu.VMEM((tm, d), BF16)],
        compiler_params=_params("parallel", "arbitrary"),
    )(x, g.reshape(1, d), w)


GLA_CHUNK = 128


def _gla_kernel(*refs, reverse, nchunk, final):
    if final:
        (q_ref, k_ref, v_ref, lr_ref, wd_ref, bd_ref, of_ref, gate_ref, gout_ref, o_ref, st_ref) = refs
    else:
        (q_ref, k_ref, v_ref, lr_ref, wd_ref, bd_ref, o_ref, st_ref) = refs
    C = GLA_CHUNK

    @pl.when(pl.program_id(1) == 0)
    def _():
        st_ref[...] = jnp.zeros_like(st_ref)

    rows = lax.broadcasted_iota(jnp.int32, (C, C), 0)
    cols = lax.broadcasted_iota(jnp.int32, (C, C), 1)
    tri = jnp.where(cols <= rows, 1.0, 0.0).astype(BF16)
    keep = (cols >= rows) if reverse else (cols <= rows)
    scale = GLA_DK ** -0.5
    order = range(nchunk - 1, -1, -1) if reverse else range(nchunk)
    for ci in order:
        sl = pl.ds(ci * C, C)
        lr_c = lr_ref[0, sl, :].astype(BF16)
        for h in range(GLA_HEADS):
            dk = slice(h * GLA_DK, (h + 1) * GLA_DK)
            dv = slice(h * GLA_DV, (h + 1) * GLA_DV)
            q = q_ref[0, sl, dk].astype(F32) * scale
            k = k_ref[0, sl, dk].astype(F32)
            v = v_ref[0, sl, dv]
            pre = _dot(lr_c, wd_ref[h]) + bd_ref[h]
            la = (jnp.minimum(pre, 0.0) - jnp.log1p(jnp.exp(-jnp.abs(pre)))) * (1.0 / GLA_GATE_NORM)
            la_hi = la.astype(BF16)
            la_lo = (la - la_hi.astype(F32)).astype(BF16)
            b = _dot(tri, la_hi) + _dot(tri, la_lo)
            tot = b[C - 1:C, :]
            st = st_ref[h]
            vt = v.astype(F32).T.astype(BF16)
            if not reverse:
                q_in = (q * jnp.exp(b)).astype(BF16)
                q_st = q_in
                k_in = (k * jnp.exp(-b)).astype(BF16)
                k_st = (k * jnp.exp(tot - b)).astype(BF16)
            else:
                c = b - la
                q_in = (q * jnp.exp(-c)).astype(BF16)
                q_st = (q * jnp.exp(tot - c)).astype(BF16)
                k_in = (k * jnp.exp(c)).astype(BF16)
                k_st = k_in
            att = jnp.where(keep, _dot_nt(q_in, k_in), 0.0)
            o = _dot(att.astype(BF16), v) + _dot_nt(q_st, st.astype(BF16))
            st_ref[h] = jnp.exp(tot) * st + _dot(vt, k_st)
            if final:
                diag = jnp.sum(q * k, axis=-1, keepdims=True)
                o = of_ref[0, sl, dv] + o - diag * v.astype(F32)
                ms = jnp.mean(o * o, axis=-1, keepdims=True)
                o = o * lax.rsqrt(ms + RMS_EPS) * gout_ref[...]
                gt = gate_ref[0, sl, dv].astype(F32)
                o = o * (gt * jax.nn.sigmoid(gt))
            o_ref[0, sl, dv] = o.astype(o_ref.dtype)


def _gla_branch(proj, lr, w_dec_f, b_dec_f, w_dec_b, b_dec_b, g_out, tb):
    bsz, t, _ = proj.shape
    tb = min(tb, t)
    nblk = t // tb
    nchunk = tb // GLA_CHUNK

    def dec_weights(w_dec, b_dec, row0):
        w = jnp.zeros((GLA_HEADS, LANES, GLA_DK), F32)
        w = w.at[:, row0:row0 + GLA_LOWRANK, :].set(
            w_dec.reshape(GLA_LOWRANK, GLA_HEADS, GLA_DK).transpose(1, 0, 2))
        return w.astype(BF16), b_dec.reshape(GLA_HEADS, 1, GLA_DK).astype(F32)

    def call(reverse, o_fwd):
        wd, bd = dec_weights(w_dec_b, b_dec_b, GLA_LOWRANK) if reverse else dec_weights(w_dec_f, b_dec_f, 0)
        blk = (lambda j: nblk - 1 - j) if reverse else (lambda j: j)
        in_specs = [
            pl.BlockSpec((1, tb, GLA_QK), lambda b, j: (b, blk(j), OFF_GQ // GLA_QK)),
            pl.BlockSpec((1, tb, GLA_QK), lambda b, j: (b, blk(j), OFF_GK // GLA_QK)),
            pl.BlockSpec((1, tb, GLA_V), lambda b, j: (b, blk(j), OFF_GV // GLA_V)),
            pl.BlockSpec((1, tb, LANES), lambda b, j: (b, blk(j), 0)),
            pl.BlockSpec((GLA_HEADS, LANES, GLA_DK), lambda b, j: (0, 0, 0)),
            pl.BlockSpec((GLA_HEADS, 1, GLA_DK), lambda b, j: (0, 0, 0)),
        ]
        args = [proj, proj, proj, lr, wd, bd]
        if reverse:
            in_specs += [
                pl.BlockSpec((1, tb, GLA_V), lambda b, j: (b, blk(j), 0)),
                pl.BlockSpec((1, tb, GLA_V), lambda b, j: (b, blk(j), OFF_GG // GLA_V)),
                pl.BlockSpec((1, GLA_DV), lambda b, j: (0, 0)),
            ]
            args += [o_fwd, proj, g_out.reshape(1, GLA_DV).astype(F32)]
        return pl.pallas_call(
            functools.partial(_gla_kernel, reverse=reverse, nchunk=nchunk, final=reverse),
            grid=(bsz, nblk),
            in_specs=in_specs,
            out_specs=pl.BlockSpec((1, tb, GLA_V), lambda b, j: (b, blk(j), 0)),
            out_shape=jax.ShapeDtypeStruct((bsz, t, GLA_V), BF16 if reverse else F32),
            scratch_shapes=[pltpu.VMEM((GLA_HEADS, GLA_DV, GLA_DK), F32)],
            compiler_params=_params("parallel", "arbitrary"),
        )(*args)

    return call(True, call(False, None))


def _group_sumsq(x, ones_bd):
    x2 = x * x
    hi = x2.astype(BF16)
    lo = (x2 - hi.astype(F32)).astype(BF16)
    pieces = []
    for c in range(x.shape[1] // LANES):
        cs = slice(c * LANES, (c + 1) * LANES)
        pieces.append(_dot(hi[:, cs], ones_bd) + _dot(lo[:, cs], ones_bd))
    return jnp.concatenate(pieces, axis=1)


def _diff_prep_kernel(q_ref, k_ref, v_ref, gq_ref, gk_ref, cos_ref, sin_ref, qz_ref, kt_ref, vh_ref):
    tm = q_ref.shape[1]
    r = lax.broadcasted_iota(jnp.int32, (LANES, LANES), 0) // DIFF_DH
    c = lax.broadcasted_iota(jnp.int32, (LANES, LANES), 1) // DIFF_DH
    ones_bd = jnp.where(r == c, 1.0, 0.0).astype(BF16)
    lane = lax.broadcasted_iota(jnp.int32, (tm, DIFF_QK), 1)
    first_half = (lane % DIFF_DH) < (DIFF_DH // 2)
    reps = DIFF_QK // LANES
    cos = jnp.concatenate([cos_ref[...]] * reps, axis=1)
    sin = jnp.concatenate([sin_ref[...]] * reps, axis=1)

    def norm_rope(x, g):
        ms = _group_sumsq(x, ones_bd) * (1.0 / DIFF_DH)
        xn = x * lax.rsqrt(ms + RMS_EPS) * g
        partner = jnp.where(first_half,
                            pltpu.roll(xn, DIFF_QK - DIFF_DH // 2, axis=1),
                            pltpu.roll(xn, DIFF_DH // 2, axis=1))
        return xn * cos + partner * sin

    qr = norm_rope(q_ref[0].astype(F32), gq_ref[...]) * (DIFF_DH ** -0.5 * LOG2E)
    kr = norm_rope(k_ref[0].astype(F32), gk_ref[...])
    lane_h = lax.broadcasted_iota(jnp.int32, (tm, LANES), 1)
    pieces = []
    for h in range(DIFF_HEADS):
        qh = qr[:, h * LANES:(h + 1) * LANES]
        pieces.append(jnp.where(lane_h < DIFF_DH, qh, 0.0))
        pieces.append(jnp.where(lane_h >= DIFF_DH, qh, 0.0))
        kt_ref[0, h] = kr[:, h * LANES:(h + 1) * LANES].T.astype(BF16)
        vh_ref[0, h] = v_ref[0, :, h * DIFF_DV:(h + 1) * DIFF_DV]
    qz_ref[0] = jnp.concatenate(pieces, axis=1).astype(BF16)


FLASH_SAFE_LOG2_RANGE = 60.0


def _flash_kernel(bound_ref, qz_ref, kt_ref, v_ref, lq1_ref, lk1_ref, lq2_ref, lk2_ref, gsub_ref, o_ref,
                  qs_ref, m_ref, l_ref, acc_ref, *, tq, tk, nk, lam_init):
    qs_ref[0:tq, :] = qz_ref[0, :, 0:LANES]
    qs_ref[tq:2 * tq, :] = qz_ref[0, :, LANES:2 * LANES]
    reps = tk // LANES
    bound = bound_ref[0]

    def scores(i):
        k0 = pl.multiple_of(i * tk, tk)
        s = _dot(qs_ref[...], kt_ref[0, 0, :, pl.ds(k0, tk)])
        return s, v_ref[0, 0, pl.ds(k0, tk), :]

    def finalize(o):
        lam = (jnp.exp(jnp.sum(lq1_ref[...] * lk1_ref[...], axis=1, keepdims=True))
               - jnp.exp(jnp.sum(lq2_ref[...] * lk2_ref[...], axis=1, keepdims=True)) + lam_init)
        o = o[0:tq] - lam * o[tq:2 * tq]
        ms = jnp.mean(o * o, axis=-1, keepdims=True)
        o = o * lax.rsqrt(ms + RMS_EPS) * gsub_ref[...] * (1.0 - lam_init)
        o_ref[0] = o.astype(o_ref.dtype)

    @pl.when(bound <= FLASH_SAFE_LOG2_RANGE)
    def _():
        l_ref[...] = jnp.zeros_like(l_ref)
        acc_ref[...] = jnp.zeros_like(acc_ref)

        def step(i, carry):
            s, v = scores(i)
            p = jnp.exp2(s - bound)
            part = p[:, 0:LANES]
            for r in range(1, reps):
                part = part + p[:, r * LANES:(r + 1) * LANES]
            l_ref[...] += part
            acc_ref[...] += _dot(p.astype(BF16), v)
            return carry

        lax.fori_loop(0, nk, step, 0)
        finalize(acc_ref[...] / jnp.sum(l_ref[...], axis=1, keepdims=True))

    @pl.when(bound > FLASH_SAFE_LOG2_RANGE)
    def _():
        m_ref[...] = jnp.full_like(m_ref, -jnp.inf)
        l_ref[...] = jnp.zeros_like(l_ref)
        acc_ref[...] = jnp.zeros_like(acc_ref)

        def step(i, carry):
            s, v = scores(i)
            m_prev = m_ref[...]
            m_new = jnp.maximum(m_prev, jnp.max(s, axis=1, keepdims=True))
            alpha = jnp.exp2(m_prev - m_new)
            p = jnp.exp2(s - jnp.concatenate([m_new] * reps, axis=1))
            l_ref[...] = alpha * l_ref[...] + jnp.sum(p, axis=1, keepdims=True)
            acc_ref[...] = alpha * acc_ref[...] + _dot(p.astype(BF16), v)
            m_ref[...] = m_new
            return carry

        lax.fori_loop(0, nk, step, 0)
        finalize(acc_ref[...] / l_ref[...])


def _rope_tables(t):
    half = DIFF_DH // 2
    inv_freq = ROPE_THETA ** (-jnp.arange(0, DIFF_DH, 2, dtype=F32) / DIFF_DH)
    ang = jnp.arange(t, dtype=F32)[:, None] * inv_freq[None, :]
    cos, sin = jnp.cos(ang), jnp.sin(ang)
    cos_t = jnp.tile(cos, (1, LANES // half))
    sin_t = jnp.tile(jnp.concatenate([-sin, sin], axis=1), (1, LANES // DIFF_DH))
    return cos_t, sin_t


def _diff_branch(proj, g_q, g_k, lq1, lk1, lq2, lk2, g_sub, lam_init, tm, tq, tk):
    bsz, t, _ = proj.shape
    tm, tq, tk = min(tm, t), min(tq, t), min(tk, t)
    cos_t, sin_t = _rope_tables(t)
    tile_g = lambda g: jnp.tile(g.astype(F32), DIFF_QK // DIFF_DH).reshape(1, DIFF_QK)
    qz, kt, vh = pl.pallas_call(
        _diff_prep_kernel,
        grid=(bsz, t // tm),
        in_specs=[pl.BlockSpec((1, tm, DIFF_QK), lambda b, i: (b, i, OFF_DQ // DIFF_QK)),
                  pl.BlockSpec((1, tm, DIFF_QK), lambda b, i: (b, i, OFF_DK // DIFF_QK)),
                  pl.BlockSpec((1, tm, DIFF_V), lambda b, i: (b, i, OFF_DV // DIFF_V)),
                  pl.BlockSpec((1, DIFF_QK), lambda b, i: (0, 0)),
                  pl.BlockSpec((1, DIFF_QK), lambda b, i: (0, 0)),
                  pl.BlockSpec((tm, LANES), lambda b, i: (i, 0)),
                  pl.BlockSpec((tm, LANES), lambda b, i: (i, 0))],
        out_specs=[pl.BlockSpec((1, tm, 2 * DIFF_QK), lambda b, i: (b, i, 0)),
                   pl.BlockSpec((1, DIFF_HEADS, LANES, tm), lambda b, i: (b, 0, 0, i)),
                   pl.BlockSpec((1, DIFF_HEADS, tm, DIFF_DV), lambda b, i: (b, 0, i, 0))],
        out_shape=[jax.ShapeDtypeStruct((bsz, t, 2 * DIFF_QK), BF16),
                   jax.ShapeDtypeStruct((bsz, DIFF_HEADS, LANES, t), BF16),
                   jax.ShapeDtypeStruct((bsz, DIFF_HEADS, t, DIFF_DV), BF16)],
        compiler_params=_params("parallel", "parallel"),
    )(proj, proj, proj, tile_g(g_q), tile_g(g_k), cos_t, sin_t)

    vec = lambda a: a.reshape(1, DIFF_DH).astype(F32)
    bound = (1.01 * DIFF_DH * (DIFF_DH ** -0.5 * LOG2E)
             * jnp.max(jnp.abs(g_q)) * jnp.max(jnp.abs(g_k))).astype(F32).reshape(1)
    return pl.pallas_call(
        functools.partial(_flash_kernel, tq=tq, tk=tk, nk=t // tk, lam_init=lam_init),
        grid=(bsz, DIFF_HEADS, t // tq),
        in_specs=[pl.BlockSpec(memory_space=pltpu.SMEM),
                  pl.BlockSpec((1, tq, 2 * LANES), lambda b, h, i: (b, i, h)),
                  pl.BlockSpec((1, 1, LANES, t), lambda b, h, i: (b, h, 0, 0)),
                  pl.BlockSpec((1, 1, t, DIFF_DV), lambda b, h, i: (b, h, 0, 0)),
                  pl.BlockSpec((1, DIFF_DH), lambda b, h, i: (0, 0)),
                  pl.BlockSpec((1, DIFF_DH), lambda b, h, i: (0, 0)),
                  pl.BlockSpec((1, DIFF_DH), lambda b, h, i: (0, 0)),
                  pl.BlockSpec((1, DIFF_DH), lambda b, h, i: (0, 0)),
                  pl.BlockSpec((1, DIFF_DV), lambda b, h, i: (0, 0))],
        out_specs=pl.BlockSpec((1, tq, DIFF_DV), lambda b, h, i: (b, i, h)),
        out_shape=jax.ShapeDtypeStruct((bsz, t, DIFF_V), BF16),
        scratch_shapes=[pltpu.VMEM((2 * tq, LANES), BF16),
                        pltpu.VMEM((2 * tq, LANES), F32),
                        pltpu.VMEM((2 * tq, LANES), F32),
                        pltpu.VMEM((2 * tq, DIFF_DV), F32)],
        compiler_params=_params("parallel", "parallel", "arbitrary"),
    )(bound, qz, kt, vh, vec(lq1), vec(lk1), vec(lq2), vec(lk2), g_sub.reshape(1, DIFF_DV).astype(F32))


def _mem_kernel(q_ref, kv_ref, gq_ref, gk_ref, o_ref):
    outs = []
    for h in range(MEM_HEADS):
        hs = slice(h * MEM_DH, (h + 1) * MEM_DH)
        q = q_ref[0, :, hs].astype(F32)
        q = q * lax.rsqrt(jnp.mean(q * q, axis=-1, keepdims=True) + RMS_EPS) * gq_ref[...]
        k = kv_ref[0, :, hs]
        k = k * lax.rsqrt(jnp.mean(k * k, axis=-1, keepdims=True) + RMS_EPS) * gk_ref[...]
        v = kv_ref[0, :, MEM_Q + h * MEM_DH:MEM_Q + (h + 1) * MEM_DH]
        s = _dot_nt(q.astype(BF16), k.astype(BF16)) * (MEM_DH ** -0.5)
        p = jnp.exp(s - jnp.max(s, axis=-1, keepdims=True))
        p = p / jnp.sum(p, axis=-1, keepdims=True)
        outs.append(_dot(p.astype(BF16), v.astype(BF16)))
    o_ref[0] = jnp.concatenate(outs, axis=1).astype(o_ref.dtype)


def _mem_branch(proj, kv, g_q, g_k, tm):
    bsz, t, _ = proj.shape
    tm = min(tm, t)
    return pl.pallas_call(
        _mem_kernel,
        grid=(bsz, t // tm),
        in_specs=[pl.BlockSpec((1, tm, MEM_Q), lambda b, i: (b, i, OFF_MQ // MEM_Q)),
                  pl.BlockSpec((1, N_MEM, 2 * MEM_Q), lambda b, i: (b, 0, 0)),
                  pl.BlockSpec((1, MEM_DH), lambda b, i: (0, 0)),
                  pl.BlockSpec((1, MEM_DH), lambda b, i: (0, 0))],
        out_specs=pl.BlockSpec((1, tm, MEM_Q), lambda b, i: (b, i, 0)),
        out_shape=jax.ShapeDtypeStruct((bsz, t, MEM_Q), BF16),
        compiler_params=_params("parallel", "parallel"),
    )(proj, kv, g_q.reshape(1, MEM_DH).astype(F32), g_k.reshape(1, MEM_DH).astype(F32))


def _merge_kernel(x_ref, pg_ref, pd_ref, pm_ref, g0_ref, g1_ref, g2_ref,
                  wg_ref, wd_ref, wm_ref, wo_ref, o_ref):
    merged = (jax.nn.sigmoid(g0_ref[...].astype(F32)) * _dot(pg_ref[...], wg_ref[...])
              + jax.nn.sigmoid(g1_ref[...].astype(F32)) * _dot(pd_ref[...], wd_ref[...])
              + jax.nn.sigmoid(g2_ref[...].astype(F32)) * _dot(pm_ref[...], wm_ref[...]))
    o_ref[...] = x_ref[...] + _dot(merged.astype(BF16), wo_ref[...])


def _merge(x, proj, pg, pd, pm, wg, wd, wm, wo, tm):
    n, d = x.shape
    tm = min(tm, n)
    row = lambda c: pl.BlockSpec((tm, d), lambda i: (i, c))
    full = pl.BlockSpec((d, d), lambda i: (0, 0))
    br = OFF_BR // d
    return pl.pallas_call(
        _merge_kernel,
        grid=(n // tm,),
        in_specs=[row(0), row(0), row(0), row(0), row(br), row(br + 1), row(br + 2), full, full, full, full],
        out_specs=row(0),
        out_shape=jax.ShapeDtypeStruct((n, d), F32),
        compiler_params=_params("parallel"),
    )(x, pg, pd, pm, proj, proj, proj, wg, wd, wm, wo)


PEER_ECHUNK = 2048
PEER_IPER = PEER_ECHUNK // PEER_NKEYS
PEER_PIECE = 2


def _bitonic_merge_desc(a):
    n = len(a)
    j = n // 2
    while j >= 1:
        for i in range(n):
            l = i ^ j
            if l > i:
                a[i], a[l] = jnp.maximum(a[i], a[l]), jnp.minimum(a[i], a[l])
        j //= 2
    return a


def _top16_desc(vals):
    a = list(vals)
    n = len(a)
    k = 2
    while k <= n:
        j = k // 2
        while j >= 1:
            for i in range(n):
                l = i ^ j
                if l > i:
                    hi, lo = jnp.maximum(a[i], a[l]), jnp.minimum(a[i], a[l])
                    a[i], a[l] = (hi, lo) if (i & k) == 0 else (lo, hi)
            j //= 2
        k *= 2
    for shift in (4, 2, 1):
        a = [jnp.maximum(a[r], pltpu.roll(a[n - 1 - r], shift, axis=0)) for r in range(n)]
        a = _bitonic_merge_desc(a)
    return a


def _pack_pair(hi, lo):
    bits = lambda x: lax.bitcast_convert_type(x.astype(BF16).astype(F32), jnp.uint32)
    return lax.bitcast_convert_type(bits(hi) | (bits(lo) >> 16), F32)


def _unpack_pair_rows(row, nrows):
    w = lax.bitcast_convert_type(jnp.broadcast_to(row, (SUBLANES, row.shape[1])), jnp.uint32)
    hi = (w & jnp.uint32(0xFFFF0000)) | (w >> 16)
    lo = (w << 16) | (w & jnp.uint32(0x0000FFFF))
    tile = lambda u: pltpu.bitcast(jnp.concatenate([u] * (nrows // (2 * SUBLANES)), axis=0), BF16)
    return tile(hi), tile(lo)


def _peer_kernel(x_ref, g_ref, wqt_ref, k1h_ref, k1l_ref, k2h_ref, k2l_ref, u_ref, vt_ref, o_ref,
                 hbt_ref, tc_ref, e2_ref, a_ref, wt_ref, acc_ref):
    ec = pl.program_id(1)
    tt = x_ref.shape[0]
    neg = jnp.full((SUBLANES, tt), -jnp.inf, F32)
    sub = lax.broadcasted_iota(jnp.int32, (SUBLANES, tt), 0)

    @pl.when(ec == 0)
    def _():
        x = x_ref[...]
        hb = x * lax.rsqrt(jnp.mean(x * x, axis=-1, keepdims=True) + RMS_EPS) * g_ref[...]
        hbt_ref[...] = hb.T.astype(BF16)
        acc_ref[...] = jnp.zeros_like(acc_ref)
        for h in range(PEER_HEADS):
            def scores(kh_ref, kl_ref, row0):
                qt = _dot(wqt_ref[row0:row0 + PEER_DQH, :], hbt_ref[...])
                qh = qt.astype(BF16)
                ql = (qt - qh.astype(F32)).astype(BF16)
                return _dot(kh_ref[...], qh) + _dot(kh_ref[...], ql) + _dot(kl_ref[...], qh)

            s1 = scores(k1h_ref, k1l_ref, h * PEER_DQ)
            s2 = scores(k2h_ref, k2l_ref, h * PEER_DQ + PEER_DQH)
            blocks = lambda s: [s[r * SUBLANES:(r + 1) * SUBLANES, :] for r in range(PEER_NKEYS // SUBLANES)]
            v1 = _top16_desc(blocks(s1))
            v2 = _top16_desc(blocks(s2))

            def pack(vs):
                out = neg
                for r in range(SUBLANES):
                    out = jnp.where(sub == r, vs[r], out)
                return out

            v2_lo, v2_hi, v1_hi = pack(v2[:SUBLANES]), pack(v2[SUBLANES:]), pack(v1[SUBLANES:])
            cand = [v1[0] + v2_lo, v1[0] + v2_hi, v1_hi + v2[0]]
            cand += [v1[a] + v2_lo for a in range(1, SUBLANES)]
            cand += [neg] * (PEER_TOPK - len(cand))
            top = _top16_desc(cand)
            tau16 = top[PEER_TOPK - 1]
            below = lambda s, bound: jnp.max(jnp.where(s < bound, s, -jnp.inf), axis=0, keepdims=True)
            tau17 = below(jnp.concatenate(cand[:10], axis=0), tau16[0:1, :])
            tau17 = jnp.maximum(tau17, below(s1, v1[PEER_TOPK - 1][0:1, :]) + v2[0][0:1, :])
            tau17 = jnp.maximum(tau17, below(s2, v2[PEER_TOPK - 1][0:1, :]) + v1[0][0:1, :])
            tau = 0.5 * (tau16[0:1, :] + tau17)
            z = top[0] - top[0]
            for r in range(PEER_TOPK):
                z = z + jnp.exp(top[r] - top[0])
            z = z[0:1, :]
            nchunks = PEER_NKEYS // PEER_IPER
            m2 = v2[0][0:1, :]
            tc_ref[h] = _pack_pair(jnp.exp(tau - s1 - m2),
                                   0.5 * jnp.exp(s1 - v1[0][0:1, :]) / z).reshape(nchunks, PEER_IPER, tt)
            e2_ref[h] = jnp.exp(s2 - m2).astype(BF16)

    half_lanes = tt // 2

    def gates(il, ls):
        acc = None
        for h in range(PEER_HEADS):
            e2v = e2_ref[h, :, ls]
            thr, coef = _unpack_pair_rows(tc_ref[h, ec, il:il + 1, ls], PEER_NKEYS)
            term = jnp.where(e2v >= thr, e2v, jnp.zeros_like(e2v)) * coef
            acc = term if acc is None else acc + term
        return acc

    def weights(il, ls, g):
        rs = slice(il * PEER_NKEYS, (il + 1) * PEER_NKEYS)
        a = a_ref[rs, ls]
        gelu2 = a * (1.0 + lax.erf(a * (2.0 ** -0.5)))
        wt_ref[rs, ls] = gelu2.astype(BF16) * g

    def value_matmul(hs):
        acc_ref[:, hs] += _dot(vt_ref[0], wt_ref[:, hs])

    lane_blocks = [slice(lb * LANES, (lb + 1) * LANES) for lb in range(tt // LANES)]
    first, second = lane_blocks[:len(lane_blocks) // 2], lane_blocks[len(lane_blocks) // 2:]
    for lanes, hs in ((first, slice(0, half_lanes)), (second, slice(half_lanes, tt))):
        for piece in range(PEER_IPER // PEER_PIECE):
            if lanes is first:
                ps = slice(piece * PEER_PIECE * PEER_NKEYS, (piece + 1) * PEER_PIECE * PEER_NKEYS)
                a_ref[ps, :] = _dot(u_ref[ps, :], hbt_ref[...])
            for il in range(piece * PEER_PIECE, (piece + 1) * PEER_PIECE):
                for ls in lanes:
                    weights(il, ls, gates(il, ls))
        value_matmul(hs)

    @pl.when(ec == pl.num_programs(1) - 1)
    def _():
        o_ref[...] = x_ref[...] + acc_ref[...].T


def _peer(x, g_ffn, wqt, k1h, k1l, k2h, k2l, u, vt, tt):
    n, d = x.shape
    tt = min(tt, n)
    nec = PEER_EXPERTS // PEER_ECHUNK
    const = lambda shape: pl.BlockSpec(shape, lambda i, e: (0,) * len(shape))
    key_spec = const((PEER_NKEYS, PEER_DQH))
    assert tt % (2 * LANES) == 0 and n % tt == 0
    stat_i = pltpu.VMEM((PEER_HEADS, PEER_NKEYS // PEER_IPER, PEER_IPER, tt), F32)
    return pl.pallas_call(
        _peer_kernel,
        grid=(n // tt, nec),
        in_specs=[pl.BlockSpec((tt, d), lambda i, e: (i, 0)),
                  const((1, d)),
                  const((PEER_HEADS * PEER_DQ, d)),
                  key_spec, key_spec, key_spec, key_spec,
                  pl.BlockSpec((PEER_ECHUNK, d), lambda i, e: (e, 0)),
                  pl.BlockSpec((1, d, PEER_ECHUNK), lambda i, e: (e, 0, 0))],
        out_specs=pl.BlockSpec((tt, d), lambda i, e: (i, 0)),
        out_shape=jax.ShapeDtypeStruct((n, d), F32),
        scratch_shapes=[pltpu.VMEM((d, tt), BF16), stat_i,
                        pltpu.VMEM((PEER_HEADS, PEER_NKEYS, tt), BF16),
                        pltpu.VMEM((PEER_ECHUNK, tt), F32),
                        pltpu.VMEM((PEER_ECHUNK, tt), BF16),
                        pltpu.VMEM((d, tt), F32)],
        compiler_params=_params("parallel", "arbitrary"),
    )(x, g_ffn.reshape(1, d), wqt, k1h, k1l, k2h, k2l, u, vt)


def _split_hi_lo(a):
    hi = a.astype(BF16)
    return hi, (a - hi.astype(F32)).astype(BF16)


def _prepare_weights(g_mix, w_in, gla_w_dec_f, gla_b_dec_f, gla_w_dec_b, gla_b_dec_b, gla_g_out, gla_w_o,
                     diff_g_q, diff_g_k, diff_lq1, diff_lk1, diff_lq2, diff_lk2, diff_g_sub, diff_w_o,
                     mem_g_norm, mem_w_kv, mem_g_q, mem_g_k, mem_w_o, w_out,
                     g_ffn, peer_w_q, peer_sub_k1, peer_sub_k2, peer_u, peer_v):
    w = w_in[0]
    lr0 = 2 * GLA_QK + 2 * GLA_V
    lr1 = lr0 + 2 * GLA_LOWRANK
    w_main = jnp.concatenate([w[:, :lr0], w[:, lr1:]], axis=1).astype(BF16)
    w_lr = jnp.pad(w[:, lr0:lr1], ((0, 0), (0, LANES - 2 * GLA_LOWRANK))).astype(BF16)
    k1h, k1l = _split_hi_lo(peer_sub_k1[0])
    k2h, k2l = _split_hi_lo(peer_sub_k2[0])
    return dict(
        g_mix=g_mix[0], w_main=w_main, w_lr=w_lr,
        gla=(gla_w_dec_f[0], gla_b_dec_f[0], gla_w_dec_b[0], gla_b_dec_b[0], gla_g_out[0]),
        diff=(diff_g_q[0], diff_g_k[0], diff_lq1[0], diff_lk1[0], diff_lq2[0], diff_lk2[0], diff_g_sub[0]),
        mem_g_norm=mem_g_norm[0], mem_w_kv=mem_w_kv[0].astype(BF16), mem_gq=mem_g_q[0], mem_gk=mem_g_k[0],
        w_o=(gla_w_o[0].astype(BF16), diff_w_o[0].astype(BF16), mem_w_o[0].astype(BF16), w_out[0].astype(BF16)),
        g_ffn=g_ffn[0], wqt=peer_w_q[0].T.astype(BF16), keys=(k1h, k1l, k2h, k2l),
        u=peer_u[0].astype(BF16),
        vt=peer_v[0].astype(BF16).reshape(PEER_EXPERTS // PEER_ECHUNK, PEER_ECHUNK, D_MODEL).transpose(0, 2, 1),
    )


def _trunk(x, mem, p):
    bsz, t, d = x.shape
    n = bsz * t
    lam_init = 0.8 - 0.6 * math.exp(-0.3 * 0)
    x2 = x.reshape(n, d)
    proj = _rms_matmul(x2, p["g_mix"], p["w_main"], BF16, 1024, 2048).reshape(bsz, t, MAIN_COLS)
    lr = _rms_matmul(x2, p["g_mix"], p["w_lr"], F32, 1024, LANES).reshape(bsz, t, LANES)
    kv = _rms_matmul(mem.reshape(bsz * N_MEM, d), p["mem_g_norm"], p["mem_w_kv"], F32, 512, 2 * MEM_Q)
    kv = kv.reshape(bsz, N_MEM, 2 * MEM_Q)
    pre_gla = _gla_branch(proj, lr, *p["gla"], tb=512)
    pre_diff = _diff_branch(proj, *p["diff"], lam_init=lam_init, tm=512, tq=512, tk=1024)
    pre_mem = _mem_branch(proj, kv, p["mem_gq"], p["mem_gk"], tm=512)
    x1 = _merge(x2, proj.reshape(n, MAIN_COLS), pre_gla.reshape(n, d), pre_diff.reshape(n, d),
                pre_mem.reshape(n, d), *p["w_o"], tm=512)
    y = _peer(x1, p["g_ffn"], p["wqt"], *p["keys"], p["u"], p["vt"], tt=512)
    return y.reshape(bsz, t, d)


def kernel(x_prompt, x_sample, mem_prompt, mem_sample, g_mix, w_in, gla_w_dec_f, gla_b_dec_f, gla_w_dec_b, gla_b_dec_b, gla_g_out, gla_w_o, diff_g_q, diff_g_k, diff_lq1, diff_lk1, diff_lq2, diff_lk2, diff_g_sub, diff_w_o, mem_g_norm, mem_w_kv, mem_g_q, mem_g_k, mem_w_o, w_out, g_ffn, peer_w_q, peer_sub_k1, peer_sub_k2, peer_u, peer_v):
    p = _prepare_weights(g_mix, w_in, gla_w_dec_f, gla_b_dec_f, gla_w_dec_b, gla_b_dec_b, gla_g_out, gla_w_o,
                         diff_g_q, diff_g_k, diff_lq1, diff_lk1, diff_lq2, diff_lk2, diff_g_sub, diff_w_o,
                         mem_g_norm, mem_w_kv, mem_g_q, mem_g_k, mem_w_o, w_out,
                         g_ffn, peer_w_q, peer_sub_k1, peer_sub_k2, peer_u, peer_v)
    return (_trunk(x_prompt, mem_prompt, p), _trunk(x_sample, mem_sample, p))
```

```python
import functools
import math

import jax
import jax.numpy as jnp
from jax import lax
from jax.experimental import pallas as pl
from jax.experimental.pallas import tpu as pltpu

F32 = jnp.float32
BF16 = jnp.bfloat16

D_MODEL = 1024
N_MEM = 256
RMS_EPS = 1e-6
ROPE_THETA = 10000.0

GLA_HEADS = 4
GLA_DK = 128
GLA_DV = 256
GLA_LOWRANK = 16
GLA_GATE_NORM = 16.0
GLA_QK = GLA_HEADS * GLA_DK
GLA_V = GLA_HEADS * GLA_DV

DIFF_HEADS = 8
DIFF_DH = 64
DIFF_DV = 2 * DIFF_DH
DIFF_QK = DIFF_HEADS * 2 * DIFF_DH
DIFF_V = DIFF_HEADS * DIFF_DV

MEM_HEADS = 4
MEM_DH = 256
MEM_Q = MEM_HEADS * MEM_DH

N_BRANCH = 3
PEER_HEADS = 8
PEER_NKEYS = 128
PEER_EXPERTS = PEER_NKEYS * PEER_NKEYS
PEER_DQ = 256
PEER_DQH = PEER_DQ // 2
PEER_TOPK = 16

LANES = 128
SUBLANES = 8
VMEM_LIMIT = 56 * 1024 * 1024

OFF_GQ = 0
OFF_GK = OFF_GQ + GLA_QK
OFF_GV = OFF_GK + GLA_QK
OFF_GG = OFF_GV + GLA_V
OFF_DQ = OFF_GG + GLA_V
OFF_DK = OFF_DQ + DIFF_QK
OFF_DV = OFF_DK + DIFF_QK
OFF_MQ = OFF_DV + DIFF_V
OFF_BR = OFF_MQ + MEM_Q
MAIN_COLS = OFF_BR + N_BRANCH * D_MODEL

LOG2E = 1.4426950408889634


def _dot(a, b):
    return jnp.dot(a, b, preferred_element_type=F32)


def _dot_nt(a, b):
    return lax.dot_general(a, b, (((1,), (1,)), ((), ())), preferred_element_type=F32)


def _params(*sem):
    return pltpu.CompilerParams(dimension_semantics=sem, vmem_limit_bytes=VMEM_LIMIT)


def _rms_matmul_kernel(x_ref, g_ref, w_ref, o_ref, h_ref):
    @pl.when(pl.program_id(1) == 0)
    def _():
        x = x_ref[...]
        ms = jnp.mean(x * x, axis=-1, keepdims=True)
        h_ref[...] = (x * lax.rsqrt(ms + RMS_EPS) * g_ref[...]).astype(BF16)

    o_ref[...] = _dot(h_ref[...], w_ref[...]).astype(o_ref.dtype)


def _rms_matmul(x, g, w, out_dtype, tm, tn):
    n, d = x.shape
    m = w.shape[1]
    tm = min(tm, n)
    tn = min(tn, m)
    return pl.pallas_call(
        _rms_matmul_kernel,
        grid=(n // tm, m // tn),
        in_specs=[pl.BlockSpec((tm, d), lambda i, j: (i, 0)),
                  pl.BlockSpec((1, d), lambda i, j: (0, 0)),
                  pl.BlockSpec((d, tn), lambda i, j: (0, j))],
        out_specs=pl.BlockSpec((tm, tn), lambda i, j: (i, j)),
        out_shape=jax.ShapeDtypeStruct((n, m), out_dtype),
        scratch_shapes=[pltpu.VMEM((tm, d), BF16)],
        compiler_params=_params("parallel", "arbitrary"),
    )(x, g.reshape(1, d), w)


GLA_CHUNK = 128


def _gla_kernel(*refs, reverse, nchunk, final):
    if final:
        (q_ref, k_ref, v_ref, lr_ref, wd_ref, bd_ref, of_ref, gate_ref, gout_ref, o_ref, st_ref) = refs
    else:
        (q_ref, k_ref, v_ref, lr_ref, wd_ref, bd_ref, o_ref, st_ref) = refs
    C = GLA_CHUNK

    @pl.when(pl.program_id(1) == 0)
    def _():
        st_ref[...] = jnp.zeros_like(st_ref)

    rows = lax.broadcasted_iota(jnp.int32, (C, C), 0)
    cols = lax.broadcasted_iota(jnp.int32, (C, C), 1)
    tri = jnp.where(cols <= rows, 1.0, 0.0).astype(BF16)
    keep = (cols >= rows) if reverse else (cols <= rows)
    scale = GLA_DK ** -0.5
    order = range(nchunk - 1, -1, -1) if reverse else range(nchunk)
    for ci in order:
        sl = pl.ds(ci * C, C)
        lr_c = lr_ref[0, sl, :].astype(BF16)
        for h in range(GLA_HEADS):
            dk = slice(h * GLA_DK, (h + 1) * GLA_DK)
            dv = slice(h * GLA_DV, (h + 1) * GLA_DV)
            q = q_ref[0, sl, dk].astype(F32) * scale
            k = k_ref[0, sl, dk].astype(F32)
            v = v_ref[0, sl, dv]
            pre = _dot(lr_c, wd_ref[h]) + bd_ref[h]
            la = (jnp.minimum(pre, 0.0) - jnp.log1p(jnp.exp(-jnp.abs(pre)))) * (1.0 / GLA_GATE_NORM)
            la_hi = la.astype(BF16)
            la_lo = (la - la_hi.astype(F32)).astype(BF16)
            b = _dot(tri, la_hi) + _dot(tri, la_lo)
            tot = b[C - 1:C, :]
            st = st_ref[h]
            vt = v.astype(F32).T.astype(BF16)
            if not reverse:
                q_in = (q * jnp.exp(b)).astype(BF16)
                q_st = q_in
                k_in = (k * jnp.exp(-b)).astype(BF16)
                k_st = (k * jnp.exp(tot - b)).astype(BF16)
            else:
                c = b - la
                q_in = (q * jnp.exp(-c)).astype(BF16)
                q_st = (q * jnp.exp(tot - c)).astype(BF16)
                k_in = (k * jnp.exp(c)).astype(BF16)
                k_st = k_in
            att = jnp.where(keep, _dot_nt(q_in, k_in), 0.0)
            o = _dot(att.astype(BF16), v) + _dot_nt(q_st, st.astype(BF16))
            st_ref[h] = jnp.exp(tot) * st + _dot(vt, k_st)
            if final:
                diag = jnp.sum(q * k, axis=-1, keepdims=True)
                o = of_ref[0, sl, dv] + o - diag * v.astype(F32)
                ms = jnp.mean(o * o, axis=-1, keepdims=True)
                o = o * lax.rsqrt(ms + RMS_EPS) * gout_ref[...]
                gt = gate_ref[0, sl, dv].astype(F32)
                o = o * (gt * jax.nn.sigmoid(gt))
            o_ref[0, sl, dv] = o.astype(o_ref.dtype)


def _gla_branch(proj, lr, w_dec_f, b_dec_f, w_dec_b, b_dec_b, g_out, tb):
    bsz, t, _ = proj.shape
    tb = min(tb, t)
    nblk = t // tb
    nchunk = tb // GLA_CHUNK

    def dec_weights(w_dec, b_dec, row0):
        w = jnp.zeros((GLA_HEADS, LANES, GLA_DK), F32)
        w = w.at[:, row0:row0 + GLA_LOWRANK, :].set(
            w_dec.reshape(GLA_LOWRANK, GLA_HEADS, GLA_DK).transpose(1, 0, 2))
        return w.astype(BF16), b_dec.reshape(GLA_HEADS, 1, GLA_DK).astype(F32)

    def call(reverse, o_fwd):
        wd, bd = dec_weights(w_dec_b, b_dec_b, GLA_LOWRANK) if reverse else dec_weights(w_dec_f, b_dec_f, 0)
        blk = (lambda j: nblk - 1 - j) if reverse else (lambda j: j)
        in_specs = [
            pl.BlockSpec((1, tb, GLA_QK), lambda b, j: (b, blk(j), OFF_GQ // GLA_QK)),
            pl.BlockSpec((1, tb, GLA_QK), lambda b, j: (b, blk(j), OFF_GK // GLA_QK)),
            pl.BlockSpec((1, tb, GLA_V), lambda b, j: (b, blk(j), OFF_GV // GLA_V)),
            pl.BlockSpec((1, tb, LANES), lambda b, j: (b, blk(j), 0)),
            pl.BlockSpec((GLA_HEADS, LANES, GLA_DK), lambda b, j: (0, 0, 0)),
            pl.BlockSpec((GLA_HEADS, 1, GLA_DK), lambda b, j: (0, 0, 0)),
        ]
        args = [proj, proj, proj, lr, wd, bd]
        if reverse:
            in_specs += [
                pl.BlockSpec((1, tb, GLA_V), lambda b, j: (b, blk(j), 0)),
                pl.BlockSpec((1, tb, GLA_V), lambda b, j: (b, blk(j), OFF_GG // GLA_V)),
                pl.BlockSpec((1, GLA_DV), lambda b, j: (0, 0)),
            ]
            args += [o_fwd, proj, g_out.reshape(1, GLA_DV).astype(F32)]
        return pl.pallas_call(
            functools.partial(_gla_kernel, reverse=reverse, nchunk=nchunk, final=reverse),
            grid=(bsz, nblk),
            in_specs=in_specs,
            out_specs=pl.BlockSpec((1, tb, GLA_V), lambda b, j: (b, blk(j), 0)),
            out_shape=jax.ShapeDtypeStruct((bsz, t, GLA_V), BF16 if reverse else F32),
            scratch_shapes=[pltpu.VMEM((GLA_HEADS, GLA_DV, GLA_DK), F32)],
            compiler_params=_params("parallel", "arbitrary"),
        )(*args)

    return call(True, call(False, None))


def _group_sumsq(x, ones_bd):
    x2 = x * x
    hi = x2.astype(BF16)
    lo = (x2 - hi.astype(F32)).astype(BF16)
    pieces = []
    for c in range(x.shape[1] // LANES):
        cs = slice(c * LANES, (c + 1) * LANES)
        pieces.append(_dot(hi[:, cs], ones_bd) + _dot(lo[:, cs], ones_bd))
    return jnp.concatenate(pieces, axis=1)


def _diff_prep_kernel(q_ref, k_ref, v_ref, gq_ref, gk_ref, cos_ref, sin_ref, qz_ref, kt_ref, vh_ref):
    tm = q_ref.shape[1]
    r = lax.broadcasted_iota(jnp.int32, (LANES, LANES), 0) // DIFF_DH
    c = lax.broadcasted_iota(jnp.int32, (LANES, LANES), 1) // DIFF_DH
    ones_bd = jnp.where(r == c, 1.0, 0.0).astype(BF16)
    lane = lax.broadcasted_iota(jnp.int32, (tm, DIFF_QK), 1)
    first_half = (lane % DIFF_DH) < (DIFF_DH // 2)
    reps = DIFF_QK // LANES
    cos = jnp.concatenate([cos_ref[...]] * reps, axis=1)
    sin = jnp.concatenate([sin_ref[...]] * reps, axis=1)

    def norm_rope(x, g):
        ms = _group_sumsq(x, ones_bd) * (1.0 / DIFF_DH)
        xn = x * lax.rsqrt(ms + RMS_EPS) * g
        partner = jnp.where(first_half,
                            pltpu.roll(xn, DIFF_QK - DIFF_DH // 2, axis=1),
                            pltpu.roll(xn, DIFF_DH // 2, axis=1))
        return xn * cos + partner * sin

    qr = norm_rope(q_ref[0].astype(F32), gq_ref[...]) * (DIFF_DH ** -0.5 * LOG2E)
    kr = norm_rope(k_ref[0].astype(F32), gk_ref[...])
    lane_h = lax.broadcasted_iota(jnp.int32, (tm, LANES), 1)
    pieces = []
    for h in range(DIFF_HEADS):
        qh = qr[:, h * LANES:(h + 1) * LANES]
        pieces.append(jnp.where(lane_h < DIFF_DH, qh, 0.0))
        pieces.append(jnp.where(lane_h >= DIFF_DH, qh, 0.0))
        kt_ref[0, h] = kr[:, h * LANES:(h + 1) * LANES].T.astype(BF16)
        vh_ref[0, h] = v_ref[0, :, h * DIFF_DV:(h + 1) * DIFF_DV]
    qz_ref[0] = jnp.concatenate(pieces, axis=1).astype(BF16)


FLASH_SAFE_LOG2_RANGE = 60.0


def _flash_kernel(bound_ref, qz_ref, kt_ref, v_ref, lq1_ref, lk1_ref, lq2_ref, lk2_ref, gsub_ref, o_ref,
                  qs_ref, m_ref, l_ref, acc_ref, *, tq, tk, nk, lam_init):
    qs_ref[0:tq, :] = qz_ref[0, :, 0:LANES]
    qs_ref[tq:2 * tq, :] = qz_ref[0, :, LANES:2 * LANES]
    reps = tk // LANES
    bound = bound_ref[0]

    def scores(i):
        k0 = pl.multiple_of(i * tk, tk)
        s = _dot(qs_ref[...], kt_ref[0, 0, :, pl.ds(k0, tk)])
        return s, v_ref[0, 0, pl.ds(k0, tk), :]

    def finalize(o):
        lam = (jnp.exp(jnp.sum(lq1_ref[...] * lk1_ref[...], axis=1, keepdims=True))
               - jnp.exp(jnp.sum(lq2_ref[...] * lk2_ref[...], axis=1, keepdims=True)) + lam_init)
        o = o[0:tq] - lam * o[tq:2 * tq]
        ms = jnp.mean(o * o, axis=-1, keepdims=True)
        o = o * lax.rsqrt(ms + RMS_EPS) * gsub_ref[...] * (1.0 - lam_init)
        o_ref[0] = o.astype(o_ref.dtype)

    @pl.when(bound <= FLASH_SAFE_LOG2_RANGE)
    def _():
        l_ref[...] = jnp.zeros_like(l_ref)
        acc_ref[...] = jnp.zeros_like(acc_ref)

        def step(i, carry):
            s, v = scores(i)
            p = jnp.exp2(s - bound)
            part = p[:, 0:LANES]
            for r in range(1, reps):
                part = part + p[:, r * LANES:(r + 1) * LANES]
            l_ref[...] += part
            acc_ref[...] += _dot(p.astype(BF16), v)
            return carry

        lax.fori_loop(0, nk, step, 0)
        finalize(acc_ref[...] / jnp.sum(l_ref[...], axis=1, keepdims=True))

    @pl.when(bound > FLASH_SAFE_LOG2_RANGE)
    def _():
        m_ref[...] = jnp.full_like(m_ref, -jnp.inf)
        l_ref[...] = jnp.zeros_like(l_ref)
        acc_ref[...] = jnp.zeros_like(acc_ref)

        def step(i, carry):
            s, v = scores(i)
            m_prev = m_ref[...]
            m_new = jnp.maximum(m_prev, jnp.max(s, axis=1, keepdims=True))
            alpha = jnp.exp2(m_prev - m_new)
            p = jnp.exp2(s - jnp.concatenate([m_new] * reps, axis=1))
            l_ref[...] = alpha * l_ref[...] + jnp.sum(p, axis=1, keepdims=True)
            acc_ref[...] = alpha * acc_ref[...] + _dot(p.astype(BF16), v)
            m_ref[...] = m_new
            return carry

        lax.fori_loop(0, nk, step, 0)
        finalize(acc_ref[...] / l_ref[...])


def _rope_tables(t):
    half = DIFF_DH // 2
    inv_freq = ROPE_THETA ** (-jnp.arange(0, DIFF_DH, 2, dtype=F32) / DIFF_DH)
    ang = jnp.arange(t, dtype=F32)[:, None] * inv_freq[None, :]
    cos, sin = jnp.cos(ang), jnp.sin(ang)
    cos_t = jnp.tile(cos, (1, LANES // half))
    sin_t = jnp.tile(jnp.concatenate([-sin, sin], axis=1), (1, LANES // DIFF_DH))
    return cos_t, sin_t


def _diff_branch(proj, g_q, g_k, lq1, lk1, lq2, lk2, g_sub, lam_init, tm, tq, tk):
    bsz, t, _ = proj.shape
    tm, tq, tk = min(tm, t), min(tq, t), min(tk, t)
    cos_t, sin_t = _rope_tables(t)
    tile_g = lambda g: jnp.tile(g.astype(F32), DIFF_QK // DIFF_DH).reshape(1, DIFF_QK)
    qz, kt, vh = pl.pallas_call(
        _diff_prep_kernel,
        grid=(bsz, t // tm),
        in_specs=[pl.BlockSpec((1, tm, DIFF_QK), lambda b, i: (b, i, OFF_DQ // DIFF_QK)),
                  pl.BlockSpec((1, tm, DIFF_QK), lambda b, i: (b, i, OFF_DK // DIFF_QK)),
                  pl.BlockSpec((1, tm, DIFF_V), lambda b, i: (b, i, OFF_DV // DIFF_V)),
                  pl.BlockSpec((1, DIFF_QK), lambda b, i: (0, 0)),
                  pl.BlockSpec((1, DIFF_QK), lambda b, i: (0, 0)),
                  pl.BlockSpec((tm, LANES), lambda b, i: (i, 0)),
                  pl.BlockSpec((tm, LANES), lambda b, i: (i, 0))],
        out_specs=[pl.BlockSpec((1, tm, 2 * DIFF_QK), lambda b, i: (b, i, 0)),
                   pl.BlockSpec((1, DIFF_HEADS, LANES, tm), lambda b, i: (b, 0, 0, i)),
                   pl.BlockSpec((1, DIFF_HEADS, tm, DIFF_DV), lambda b, i: (b, 0, i, 0))],
        out_shape=[jax.ShapeDtypeStruct((bsz, t, 2 * DIFF_QK), BF16),
                   jax.ShapeDtypeStruct((bsz, DIFF_HEADS, LANES, t), BF16),
                   jax.ShapeDtypeStruct((bsz, DIFF_HEADS, t, DIFF_DV), BF16)],
        compiler_params=_params("parallel", "parallel"),
    )(proj, proj, proj, tile_g(g_q), tile_g(g_k), cos_t, sin_t)

    vec = lambda a: a.reshape(1, DIFF_DH).astype(F32)
    bound = (1.01 * DIFF_DH * (DIFF_DH ** -0.5 * LOG2E)
             * jnp.max(jnp.abs(g_q)) * jnp.max(jnp.abs(g_k))).astype(F32).reshape(1)
    return pl.pallas_call(
        functools.partial(_flash_kernel, tq=tq, tk=tk, nk=t // tk, lam_init=lam_init),
        grid=(bsz, DIFF_HEADS, t // tq),
        in_specs=[pl.BlockSpec(memory_space=pltpu.SMEM),
                  pl.BlockSpec((1, tq, 2 * LANES), lambda b, h, i: (b, i, h)),
                  pl.BlockSpec((1, 1, LANES, t), lambda b, h, i: (b, h, 0, 0)),
                  pl.BlockSpec((1, 1, t, DIFF_DV), lambda b, h, i: (b, h, 0, 0)),
                  pl.BlockSpec((1, DIFF_DH), lambda b, h, i: (0, 0)),
                  pl.BlockSpec((1, DIFF_DH), lambda b, h, i: (0, 0)),
                  pl.BlockSpec((1, DIFF_DH), lambda b, h, i: (0, 0)),
                  pl.BlockSpec((1, DIFF_DH), lambda b, h, i: (0, 0)),
                  pl.BlockSpec((1, DIFF_DV), lambda b, h, i: (0, 0))],
        out_specs=pl.BlockSpec((1, tq, DIFF_DV), lambda b, h, i: (b, i, h)),
        out_shape=jax.ShapeDtypeStruct((bsz, t, DIFF_V), BF16),
        scratch_shapes=[pltpu.VMEM((2 * tq, LANES), BF16),
                        pltpu.VMEM((2 * tq, LANES), F32),
                        pltpu.VMEM((2 * tq, LANES), F32),
                        pltpu.VMEM((2 * tq, DIFF_DV), F32)],
        compiler_params=_params("parallel", "parallel", "arbitrary"),
    )(bound, qz, kt, vh, vec(lq1), vec(lk1), vec(lq2), vec(lk2), g_sub.reshape(1, DIFF_DV).astype(F32))


def _mem_kernel(q_ref, kv_ref, gq_ref, gk_ref, o_ref):
    outs = []
    for h in range(MEM_HEADS):
        hs = slice(h * MEM_DH, (h + 1) * MEM_DH)
        q = q_ref[0, :, hs].astype(F32)
        q = q * lax.rsqrt(jnp.mean(q * q, axis=-1, keepdims=True) + RMS_EPS) * gq_ref[...]
        k = kv_ref[0, :, hs]
        k = k * lax.rsqrt(jnp.mean(k * k, axis=-1, keepdims=True) + RMS_EPS) * gk_ref[...]
        v = kv_ref[0, :, MEM_Q + h * MEM_DH:MEM_Q + (h + 1) * MEM_DH]
        s = _dot_nt(q.astype(BF16), k.astype(BF16)) * (MEM_DH ** -0.5)
        p = jnp.exp(s - jnp.max(s, axis=-1, keepdims=True))
        p = p / jnp.sum(p, axis=-1, keepdims=True)
        outs.append(_dot(p.astype(BF16), v.astype(BF16)))
    o_ref[0] = jnp.concatenate(outs, axis=1).astype(o_ref.dtype)


def _mem_branch(proj, kv, g_q, g_k, tm):
    bsz, t, _ = proj.shape
    tm = min(tm, t)
    return pl.pallas_call(
        _mem_kernel,
        grid=(bsz, t // tm),
        in_specs=[pl.BlockSpec((1, tm, MEM_Q), lambda b, i: (b, i, OFF_MQ // MEM_Q)),
                  pl.BlockSpec((1, N_MEM, 2 * MEM_Q), lambda b, i: (b, 0, 0)),
                  pl.BlockSpec((1, MEM_DH), lambda b, i: (0, 0)),
                  pl.BlockSpec((1, MEM_DH), lambda b, i: (0, 0))],
        out_specs=pl.BlockSpec((1, tm, MEM_Q), lambda b, i: (b, i, 0)),
        out_shape=jax.ShapeDtypeStruct((bsz, t, MEM_Q), BF16),
        compiler_params=_params("parallel", "parallel"),
    )(proj, kv, g_q.reshape(1, MEM_DH).astype(F32), g_k.reshape(1, MEM_DH).astype(F32))


def _merge_kernel(x_ref, pg_ref, pd_ref, pm_ref, g0_ref, g1_ref, g2_ref,
                  wg_ref, wd_ref, wm_ref, wo_ref, o_ref):
    merged = (jax.nn.sigmoid(g0_ref[...].astype(F32)) * _dot(pg_ref[...], wg_ref[...])
              + jax.nn.sigmoid(g1_ref[...].astype(F32)) * _dot(pd_ref[...], wd_ref[...])
              + jax.nn.sigmoid(g2_ref[...].astype(F32)) * _dot(pm_ref[...], wm_ref[...]))
    o_ref[...] = x_ref[...] + _dot(merged.astype(BF16), wo_ref[...])


def _merge(x, proj, pg, pd, pm, wg, wd, wm, wo, tm):
    n, d = x.shape
    tm = min(tm, n)
    row = lambda c: pl.BlockSpec((tm, d), lambda i: (i, c))
    full = pl.BlockSpec((d, d), lambda i: (0, 0))
    br = OFF_BR // d
    return pl.pallas_call(
        _merge_kernel,
        grid=(n // tm,),
        in_specs=[row(0), row(0), row(0), row(0), row(br), row(br + 1), row(br + 2), full, full, full, full],
        out_specs=row(0),
        out_shape=jax.ShapeDtypeStruct((n, d), F32),
        compiler_params=_params("parallel"),
    )(x, pg, pd, pm, proj, proj, proj, wg, wd, wm, wo)


PEER_ECHUNK = 2048
PEER_IPER = PEER_ECHUNK // PEER_NKEYS
PEER_PIECE = 2


def _bitonic_merge_desc(a):
    n = len(a)
    j = n // 2
    while j >= 1:
        for i in range(n):
            l = i ^ j
            if l > i:
                a[i], a[l] = jnp.maximum(a[i], a[l]), jnp.minimum(a[i], a[l])
        j //= 2
    return a


def _top16_desc(vals):
    a = list(vals)
    n = len(a)
    k = 2
    while k <= n:
        j = k // 2
        while j >= 1:
            for i in range(n):
                l = i ^ j
                if l > i:
                    hi, lo = jnp.maximum(a[i], a[l]), jnp.minimum(a[i], a[l])
                    a[i], a[l] = (hi, lo) if (i & k) == 0 else (lo, hi)
            j //= 2
        k *= 2
    for shift in (4, 2, 1):
        a = [jnp.maximum(a[r], pltpu.roll(a[n - 1 - r], shift, axis=0)) for r in range(n)]
        a = _bitonic_merge_desc(a)
    return a


def _peer_kernel(x_ref, g_ref, wqt_ref, k1h_ref, k1l_ref, k2h_ref, k2l_ref, u_ref, vt_ref, o_ref,
                 hbt_ref, t_ref, c_ref, e2_ref, a_ref, wt_ref, acc_ref):
    ec = pl.program_id(1)
    tt = x_ref.shape[0]
    neg = jnp.full((SUBLANES, tt), -jnp.inf, F32)
    sub = lax.broadcasted_iota(jnp.int32, (SUBLANES, tt), 0)

    @pl.when(ec == 0)
    def _():
        x = x_ref[...]
        hb = x * lax.rsqrt(jnp.mean(x * x, axis=-1, keepdims=True) + RMS_EPS) * g_ref[...]
        hbt_ref[...] = hb.T.astype(BF16)
        acc_ref[...] = jnp.zeros_like(acc_ref)
        for h in range(PEER_HEADS):
            def scores(kh_ref, kl_ref, row0):
                qt = _dot(wqt_ref[row0:row0 + PEER_DQH, :], hbt_ref[...])
                qh = qt.astype(BF16)
                ql = (qt - qh.astype(F32)).astype(BF16)
                return _dot(kh_ref[...], qh) + _dot(kh_ref[...], ql) + _dot(kl_ref[...], qh)

            s1 = scores(k1h_ref, k1l_ref, h * PEER_DQ)
            s2 = scores(k2h_ref, k2l_ref, h * PEER_DQ + PEER_DQH)
            blocks = lambda s: [s[r * SUBLANES:(r + 1) * SUBLANES, :] for r in range(PEER_NKEYS // SUBLANES)]
            v1 = _top16_desc(blocks(s1))
            v2 = _top16_desc(blocks(s2))

            def pack(vs):
                out = neg
                for r in range(SUBLANES):
                    out = jnp.where(sub == r, vs[r], out)
                return out

            v2_lo, v2_hi, v1_hi = pack(v2[:SUBLANES]), pack(v2[SUBLANES:]), pack(v1[SUBLANES:])
            cand = [v1[0] + v2_lo, v1[0] + v2_hi, v1_hi + v2[0]]
            cand += [v1[a] + v2_lo for a in range(1, SUBLANES)]
            cand += [neg] * (PEER_TOPK - len(cand))
            top = _top16_desc(cand)
            tau16 = top[PEER_TOPK - 1]
            below = lambda s, bound: jnp.max(jnp.where(s < bound, s, -jnp.inf), axis=0, keepdims=True)
            tau17 = below(jnp.concatenate(cand[:10], axis=0), tau16[0:1, :])
            tau17 = jnp.maximum(tau17, below(s1, v1[PEER_TOPK - 1][0:1, :]) + v2[0][0:1, :])
            tau17 = jnp.maximum(tau17, below(s2, v2[PEER_TOPK - 1][0:1, :]) + v1[0][0:1, :])
            tau = 0.5 * (tau16[0:1, :] + tau17)
            z = top[0] - top[0]
            for r in range(PEER_TOPK):
                z = z + jnp.exp(top[r] - top[0])
            z = z[0:1, :]
            nchunks = PEER_NKEYS // PEER_IPER
            m2 = v2[0][0:1, :]
            t_ref[h] = jnp.exp(tau - s1 - m2).reshape(nchunks, PEER_IPER, tt)
            c_ref[h] = (0.5 * jnp.exp(s1 - v1[0][0:1, :]) / z).reshape(nchunks, PEER_IPER, tt)
            e2_ref[h] = jnp.exp(s2 - m2)

    half_lanes = tt // 2

    def gates(il, ls):
        acc = None
        for h in range(PEER_HEADS):
            e2v = e2_ref[h, :, ls]
            thr = t_ref[h, ec, il:il + 1, ls]
            coef = c_ref[h, ec, il:il + 1, ls]
            term = jnp.where(e2v >= thr, e2v, jnp.zeros_like(e2v)) * coef
            acc = term if acc is None else acc + term
        return acc

    def weights(il, ls, g):
        rs = slice(il * PEER_NKEYS, (il + 1) * PEER_NKEYS)
        a = a_ref[rs, ls]
        gelu2 = a * (1.0 + lax.erf(a * (2.0 ** -0.5)))
        wt_ref[rs, ls] = (gelu2 * g).astype(BF16)

    def value_matmul(hs):
        acc_ref[:, hs] += _dot(vt_ref[0], wt_ref[:, hs])

    lane_blocks = [slice(lb * LANES, (lb + 1) * LANES) for lb in range(tt // LANES)]
    first, second = lane_blocks[:len(lane_blocks) // 2], lane_blocks[len(lane_blocks) // 2:]
    for lanes, hs in ((first, slice(0, half_lanes)), (second, slice(half_lanes, tt))):
        for piece in range(PEER_IPER // PEER_PIECE):
            if lanes is first:
                ps = slice(piece * PEER_PIECE * PEER_NKEYS, (piece + 1) * PEER_PIECE * PEER_NKEYS)
                a_ref[ps, :] = _dot(u_ref[ps, :], hbt_ref[...])
            for il in range(piece * PEER_PIECE, (piece + 1) * PEER_PIECE):
                for ls in lanes:
                    weights(il, ls, gates(il, ls))
        value_matmul(hs)

    @pl.when(ec == pl.num_programs(1) - 1)
    def _():
        o_ref[...] = x_ref[...] + acc_ref[...].T


def _peer(x, g_ffn, wqt, k1h, k1l, k2h, k2l, u, vt, tt):
    n, d = x.shape
    tt = min(tt, n)
    nec = PEER_EXPERTS // PEER_ECHUNK
    const = lambda shape: pl.BlockSpec(shape, lambda i, e: (0,) * len(shape))
    key_spec = const((PEER_NKEYS, PEER_DQH))
    assert tt % (2 * LANES) == 0 and n % tt == 0
    stat_i = pltpu.VMEM((PEER_HEADS, PEER_NKEYS // PEER_IPER, PEER_IPER, tt), F32)
    return pl.pallas_call(
        _peer_kernel,
        grid=(n // tt, nec),
        in_specs=[pl.BlockSpec((tt, d), lambda i, e: (i, 0)),
                  const((1, d)),
                  const((PEER_HEADS * PEER_DQ, d)),
                  key_spec, key_spec, key_spec, key_spec,
                  pl.BlockSpec((PEER_ECHUNK, d), lambda i, e: (e, 0)),
                  pl.BlockSpec((1, d, PEER_ECHUNK), lambda i, e: (e, 0, 0))],
        out_specs=pl.BlockSpec((tt, d), lambda i, e: (i, 0)),
        out_shape=jax.ShapeDtypeStruct((n, d), F32),
        scratch_shapes=[pltpu.VMEM((d, tt), BF16), stat_i, stat_i,
                        pltpu.VMEM((PEER_HEADS, PEER_NKEYS, tt), F32),
                        pltpu.VMEM((PEER_ECHUNK, tt), F32),
                        pltpu.VMEM((PEER_ECHUNK, tt), BF16),
                        pltpu.VMEM((d, tt), F32)],
        compiler_params=_params("parallel", "arbitrary"),
    )(x, g_ffn.reshape(1, d), wqt, k1h, k1l, k2h, k2l, u, vt)


def _split_hi_lo(a):
    hi = a.astype(BF16)
    return hi, (a - hi.astype(F32)).astype(BF16)


def _prepare_weights(g_mix, w_in, gla_w_dec_f, gla_b_dec_f, gla_w_dec_b, gla_b_dec_b, gla_g_out, gla_w_o,
                     diff_g_q, diff_g_k, diff_lq1, diff_lk1, diff_lq2, diff_lk2, diff_g_sub, diff_w_o,
                     mem_g_norm, mem_w_kv, mem_g_q, mem_g_k, mem_w_o, w_out,
                     g_ffn, peer_w_q, peer_sub_k1, peer_sub_k2, peer_u, peer_v):
    w = w_in[0]
    lr0 = 2 * GLA_QK + 2 * GLA_V
    lr1 = lr0 + 2 * GLA_LOWRANK
    w_main = jnp.concatenate([w[:, :lr0], w[:, lr1:]], axis=1).astype(BF16)
    w_lr = jnp.pad(w[:, lr0:lr1], ((0, 0), (0, LANES - 2 * GLA_LOWRANK))).astype(BF16)
    k1h, k1l = _split_hi_lo(peer_sub_k1[0])
    k2h, k2l = _split_hi_lo(peer_sub_k2[0])
    return dict(
        g_mix=g_mix[0], w_main=w_main, w_lr=w_lr,
        gla=(gla_w_dec_f[0], gla_b_dec_f[0], gla_w_dec_b[0], gla_b_dec_b[0], gla_g_out[0]),
        diff=(diff_g_q[0], diff_g_k[0], diff_lq1[0], diff_lk1[0], diff_lq2[0], diff_lk2[0], diff_g_sub[0]),
        mem_g_norm=mem_g_norm[0], mem_w_kv=mem_w_kv[0].astype(BF16), mem_gq=mem_g_q[0], mem_gk=mem_g_k[0],
        w_o=(gla_w_o[0].astype(BF16), diff_w_o[0].astype(BF16), mem_w_o[0].astype(BF16), w_out[0].astype(BF16)),
        g_ffn=g_ffn[0], wqt=peer_w_q[0].T.astype(BF16), keys=(k1h, k1l, k2h, k2l),
        u=peer_u[0].astype(BF16),
        vt=peer_v[0].astype(BF16).reshape(PEER_EXPERTS // PEER_ECHUNK, PEER_ECHUNK, D_MODEL).transpose(0, 2, 1),
    )


def _trunk(x, mem, p):
    bsz, t, d = x.shape
    n = bsz * t
    lam_init = 0.8 - 0.6 * math.exp(-0.3 * 0)
    x2 = x.reshape(n, d)
    proj = _rms_matmul(x2, p["g_mix"], p["w_main"], BF16, 1024, 2048).reshape(bsz, t, MAIN_COLS)
    lr = _rms_matmul(x2, p["g_mix"], p["w_lr"], F32, 1024, LANES).reshape(bsz, t, LANES)
    kv = _rms_matmul(mem.reshape(bsz * N_MEM, d), p["mem_g_norm"], p["mem_w_kv"], F32, 512, 2 * MEM_Q)
    kv = kv.reshape(bsz, N_MEM, 2 * MEM_Q)
    pre_gla = _gla_branch(proj, lr, *p["gla"], tb=512)
    pre_diff = _diff_branch(proj, *p["diff"], lam_init=lam_init, tm=512, tq=512, tk=1024)
    pre_mem = _mem_branch(proj, kv, p["mem_gq"], p["mem_gk"], tm=512)
    x1 = _merge(x2, proj.reshape(n, MAIN_COLS), pre_gla.reshape(n, d), pre_diff.reshape(n, d),
                pre_mem.reshape(n, d), *p["w_o"], tm=512)
    y = _peer(x1, p["g_ffn"], p["wqt"], *p["keys"], p["u"], p["vt"], tt=512)
    return y.reshape(bsz, t, d)


def kernel(x_prompt, x_sample, mem_prompt, mem_sample, g_mix, w_in, gla_w_dec_f, gla_b_dec_f, gla_w_dec_b, gla_b_dec_b, gla_g_out, gla_w_o, diff_g_q, diff_g_k, diff_lq1, diff_lk1, diff_lq2, diff_lk2, diff_g_sub, diff_w_o, mem_g_norm, mem_w_kv, mem_g_q, mem_g_k, mem_w_o, w_out, g_ffn, peer_w_q, peer_sub_k1, peer_sub_k2, peer_u, peer_v):
    p = _prepare_weights(g_mix, w_in, gla_w_dec_f, gla_b_dec_f, gla_w_dec_b, gla_b_dec_b, gla_g_out, gla_w_o,
                         diff_g_q, diff_g_k, diff_lq1, diff_lk1, diff_lq2, diff_lk2, diff_g_sub, diff_w_o,
                         mem_g_norm, mem_w_kv, mem_g_q, mem_g_k, mem_w_o, w_out,
                         g_ffn, peer_w_q, peer_sub_k1, peer_sub_k2, peer_u, peer_v)
    return (_trunk(x_prompt, mem_prompt, p), _trunk(x_sample, mem_sample, p))
```

```python
import functools
import math

import jax
import jax.numpy as jnp
from jax import lax
from jax.experimental import pallas as pl
from jax.experimental.pallas import tpu as pltpu

F32 = jnp.float32
BF16 = jnp.bfloat16

D_MODEL = 1024
N_MEM = 256
RMS_EPS = 1e-6
ROPE_THETA = 10000.0

GLA_HEADS = 4
GLA_DK = 128
GLA_DV = 256
GLA_LOWRANK = 16
GLA_GATE_NORM = 16.0
GLA_QK = GLA_HEADS * GLA_DK
GLA_V = GLA_HEADS * GLA_DV

DIFF_HEADS = 8
DIFF_DH = 64
DIFF_DV = 2 * DIFF_DH
DIFF_QK = DIFF_HEADS * 2 * DIFF_DH
DIFF_V = DIFF_HEADS * DIFF_DV

MEM_HEADS = 4
MEM_DH = 256
MEM_Q = MEM_HEADS * MEM_DH

N_BRANCH = 3
PEER_HEADS = 8
PEER_NKEYS = 128
PEER_EXPERTS = PEER_NKEYS * PEER_NKEYS
PEER_DQ = 256
PEER_DQH = PEER_DQ // 2
PEER_TOPK = 16

LANES = 128
SUBLANES = 8
VMEM_LIMIT = 56 * 1024 * 1024

OFF_GQ = 0
OFF_GK = OFF_GQ + GLA_QK
OFF_GV = OFF_GK + GLA_QK
OFF_GG = OFF_GV + GLA_V
OFF_DQ = OFF_GG + GLA_V
OFF_DK = OFF_DQ + DIFF_QK
OFF_DV = OFF_DK + DIFF_QK
OFF_MQ = OFF_DV + DIFF_V
OFF_BR = OFF_MQ + MEM_Q
MAIN_COLS = OFF_BR + N_BRANCH * D_MODEL

LOG2E = 1.4426950408889634


def _dot(a, b):
    return jnp.dot(a, b, preferred_element_type=F32)


def _dot_nt(a, b):
    return lax.dot_general(a, b, (((1,), (1,)), ((), ())), preferred_element_type=F32)


def _params(*sem):
    return pltpu.CompilerParams(dimension_semantics=sem, vmem_limit_bytes=VMEM_LIMIT)


def _rms_matmul_kernel(x_ref, g_ref, w_ref, o_ref, h_ref):
    @pl.when(pl.program_id(1) == 0)
    def _():
        x = x_ref[...]
        ms = jnp.mean(x * x, axis=-1, keepdims=True)
        h_ref[...] = (x * lax.rsqrt(ms + RMS_EPS) * g_ref[...]).astype(BF16)

    o_ref[...] = _dot(h_ref[...], w_ref[...]).astype(o_ref.dtype)


def _rms_matmul(x, g, w, out_dtype, tm, tn):
    n, d = x.shape
    m = w.shape[1]
    tm = min(tm, n)
    tn = min(tn, m)
    return pl.pallas_call(
        _rms_matmul_kernel,
        grid=(n // tm, m // tn),
        in_specs=[pl.BlockSpec((tm, d), lambda i, j: (i, 0)),
                  pl.BlockSpec((1, d), lambda i, j: (0, 0)),
                  pl.BlockSpec((d, tn), lambda i, j: (0, j))],
        out_specs=pl.BlockSpec((tm, tn), lambda i, j: (i, j)),
        out_shape=jax.ShapeDtypeStruct((n, m), out_dtype),
        scratch_shapes=[pltpu.VMEM((tm, d), BF16)],
        compiler_params=_params("parallel", "arbitrary"),
    )(x, g.reshape(1, d), w)


GLA_CHUNK = 128


def _gla_kernel(*refs, reverse, nchunk, final):
    if final:
        (q_ref, k_ref, v_ref, lr_ref, wd_ref, bd_ref, of_ref, gate_ref, gout_ref, o_ref, st_ref) = refs
    else:
        (q_ref, k_ref, v_ref, lr_ref, wd_ref, bd_ref, o_ref, st_ref) = refs
    C = GLA_CHUNK

    @pl.when(pl.program_id(1) == 0)
    def _():
        st_ref[...] = jnp.zeros_like(st_ref)

    rows = lax.broadcasted_iota(jnp.int32, (C, C), 0)
    cols = lax.broadcasted_iota(jnp.int32, (C, C), 1)
    tri = jnp.where(cols <= rows, 1.0, 0.0).astype(BF16)
    keep = (cols >= rows) if reverse else (cols <= rows)
    scale = GLA_DK ** -0.5
    order = range(nchunk - 1, -1, -1) if reverse else range(nchunk)
    for ci in order:
        sl = pl.ds(ci * C, C)
        lr_c = lr_ref[0, sl, :].astype(BF16)
        for h in range(GLA_HEADS):
            dk = slice(h * GLA_DK, (h + 1) * GLA_DK)
            dv = slice(h * GLA_DV, (h + 1) * GLA_DV)
            q = q_ref[0, sl, dk].astype(F32) * scale
            k = k_ref[0, sl, dk].astype(F32)
            v = v_ref[0, sl, dv]
            pre = _dot(lr_c, wd_ref[h]) + bd_ref[h]
            la = (jnp.minimum(pre, 0.0) - jnp.log1p(jnp.exp(-jnp.abs(pre)))) * (1.0 / GLA_GATE_NORM)
            la_hi = la.astype(BF16)
            la_lo = (la - la_hi.astype(F32)).astype(BF16)
            b = _dot(tri, la_hi) + _dot(tri, la_lo)
            tot = b[C - 1:C, :]
            st = st_ref[h]
            vt = v.astype(F32).T.astype(BF16)
            if not reverse:
                q_in = (q * jnp.exp(b)).astype(BF16)
                q_st = q_in
                k_in = (k * jnp.exp(-b)).astype(BF16)
                k_st = (k * jnp.exp(tot - b)).astype(BF16)
            else:
                c = b - la
                q_in = (q * jnp.exp(-c)).astype(BF16)
                q_st = (q * jnp.exp(tot - c)).astype(BF16)
                k_in = (k * jnp.exp(c)).astype(BF16)
                k_st = k_in
            att = jnp.where(keep, _dot_nt(q_in, k_in), 0.0)
            o = _dot(att.astype(BF16), v) + _dot_nt(q_st, st.astype(BF16))
            st_ref[h] = jnp.exp(tot) * st + _dot(vt, k_st)
            if final:
                diag = jnp.sum(q * k, axis=-1, keepdims=True)
                o = of_ref[0, sl, dv] + o - diag * v.astype(F32)
                ms = jnp.mean(o * o, axis=-1, keepdims=True)
                o = o * lax.rsqrt(ms + RMS_EPS) * gout_ref[...]
                gt = gate_ref[0, sl, dv].astype(F32)
                o = o * (gt * jax.nn.sigmoid(gt))
            o_ref[0, sl, dv] = o.astype(o_ref.dtype)


def _gla_branch(proj, lr, w_dec_f, b_dec_f, w_dec_b, b_dec_b, g_out, tb):
    bsz, t, _ = proj.shape
    tb = min(tb, t)
    nblk = t // tb
    nchunk = tb // GLA_CHUNK

    def dec_weights(w_dec, b_dec, row0):
        w = jnp.zeros((GLA_HEADS, LANES, GLA_DK), F32)
        w = w.at[:, row0:row0 + GLA_LOWRANK, :].set(
            w_dec.reshape(GLA_LOWRANK, GLA_HEADS, GLA_DK).transpose(1, 0, 2))
        return w.astype(BF16), b_dec.reshape(GLA_HEADS, 1, GLA_DK).astype(F32)

    def call(reverse, o_fwd):
        wd, bd = dec_weights(w_dec_b, b_dec_b, GLA_LOWRANK) if reverse else dec_weights(w_dec_f, b_dec_f, 0)
        blk = (lambda j: nblk - 1 - j) if reverse else (lambda j: j)
        in_specs = [
            pl.BlockSpec((1, tb, GLA_QK), lambda b, j: (b, blk(j), OFF_GQ // GLA_QK)),
            pl.BlockSpec((1, tb, GLA_QK), lambda b, j: (b, blk(j), OFF_GK // GLA_QK)),
            pl.BlockSpec((1, tb, GLA_V), lambda b, j: (b, blk(j), OFF_GV // GLA_V)),
            pl.BlockSpec((1, tb, LANES), lambda b, j: (b, blk(j), 0)),
            pl.BlockSpec((GLA_HEADS, LANES, GLA_DK), lambda b, j: (0, 0, 0)),
            pl.BlockSpec((GLA_HEADS, 1, GLA_DK), lambda b, j: (0, 0, 0)),
        ]
        args = [proj, proj, proj, lr, wd, bd]
        if reverse:
            in_specs += [
                pl.BlockSpec((1, tb, GLA_V), lambda b, j: (b, blk(j), 0)),
                pl.BlockSpec((1, tb, GLA_V), lambda b, j: (b, blk(j), OFF_GG // GLA_V)),
                pl.BlockSpec((1, GLA_DV), lambda b, j: (0, 0)),
            ]
            args += [o_fwd, proj, g_out.reshape(1, GLA_DV).astype(F32)]
        return pl.pallas_call(
            functools.partial(_gla_kernel, reverse=reverse, nchunk=nchunk, final=reverse),
            grid=(bsz, nblk),
            in_specs=in_specs,
            out_specs=pl.BlockSpec((1, tb, GLA_V), lambda b, j: (b, blk(j), 0)),
            out_shape=jax.ShapeDtypeStruct((bsz, t, GLA_V), BF16 if reverse else F32),
            scratch_shapes=[pltpu.VMEM((GLA_HEADS, GLA_DV, GLA_DK), F32)],
            compiler_params=_params("parallel", "arbitrary"),
        )(*args)

    return call(True, call(False, None))


def _group_sumsq(x, ones_bd):
    x2 = x * x
    hi = x2.astype(BF16)
    lo = (x2 - hi.astype(F32)).astype(BF16)
    pieces = []
    for c in range(x.shape[1] // LANES):
        cs = slice(c * LANES, (c + 1) * LANES)
        pieces.append(_dot(hi[:, cs], ones_bd) + _dot(lo[:, cs], ones_bd))
    return jnp.concatenate(pieces, axis=1)


def _diff_prep_kernel(q_ref, k_ref, v_ref, gq_ref, gk_ref, cos_ref, sin_ref, qz_ref, kt_ref, vh_ref):
    tm = q_ref.shape[1]
    r = lax.broadcasted_iota(jnp.int32, (LANES, LANES), 0) // DIFF_DH
    c = lax.broadcasted_iota(jnp.int32, (LANES, LANES), 1) // DIFF_DH
    ones_bd = jnp.where(r == c, 1.0, 0.0).astype(BF16)
    lane = lax.broadcasted_iota(jnp.int32, (tm, DIFF_QK), 1)
    first_half = (lane % DIFF_DH) < (DIFF_DH // 2)
    reps = DIFF_QK // LANES
    cos = jnp.concatenate([cos_ref[...]] * reps, axis=1)
    sin = jnp.concatenate([sin_ref[...]] * reps, axis=1)

    def norm_rope(x, g):
        ms = _group_sumsq(x, ones_bd) * (1.0 / DIFF_DH)
        xn = x * lax.rsqrt(ms + RMS_EPS) * g
        partner = jnp.where(first_half,
                            pltpu.roll(xn, DIFF_QK - DIFF_DH // 2, axis=1),
                            pltpu.roll(xn, DIFF_DH // 2, axis=1))
        return xn * cos + partner * sin

    qr = norm_rope(q_ref[0].astype(F32), gq_ref[...]) * (DIFF_DH ** -0.5 * LOG2E)
    kr = norm_rope(k_ref[0].astype(F32), gk_ref[...])
    lane_h = lax.broadcasted_iota(jnp.int32, (tm, LANES), 1)
    pieces = []
    for h in range(DIFF_HEADS):
        qh = qr[:, h * LANES:(h + 1) * LANES]
        pieces.append(jnp.where(lane_h < DIFF_DH, qh, 0.0))
        pieces.append(jnp.where(lane_h >= DIFF_DH, qh, 0.0))
        kt_ref[0, h] = kr[:, h * LANES:(h + 1) * LANES].T.astype(BF16)
        vh_ref[0, h] = v_ref[0, :, h * DIFF_DV:(h + 1) * DIFF_DV]
    qz_ref[0] = jnp.concatenate(pieces, axis=1).astype(BF16)


FLASH_SAFE_LOG2_RANGE = 60.0


def _flash_kernel(bound_ref, qz_ref, kt_ref, v_ref, lq1_ref, lk1_ref, lq2_ref, lk2_ref, gsub_ref, o_ref,
                  qs_ref, m_ref, l_ref, acc_ref, *, tq, tk, nk, lam_init):
    qs_ref[0:tq, :] = qz_ref[0, :, 0:LANES]
    qs_ref[tq:2 * tq, :] = qz_ref[0, :, LANES:2 * LANES]
    reps = tk // LANES
    bound = bound_ref[0]

    def scores(i):
        k0 = pl.multiple_of(i * tk, tk)
        s = _dot(qs_ref[...], kt_ref[0, 0, :, pl.ds(k0, tk)])
        return s, v_ref[0, 0, pl.ds(k0, tk), :]

    def finalize(o):
        lam = (jnp.exp(jnp.sum(lq1_ref[...] * lk1_ref[...], axis=1, keepdims=True))
               - jnp.exp(jnp.sum(lq2_ref[...] * lk2_ref[...], axis=1, keepdims=True)) + lam_init)
        o = o[0:tq] - lam * o[tq:2 * tq]
        ms = jnp.mean(o * o, axis=-1, keepdims=True)
        o = o * lax.rsqrt(ms + RMS_EPS) * gsub_ref[...] * (1.0 - lam_init)
        o_ref[0] = o.astype(o_ref.dtype)

    @pl.when(bound <= FLASH_SAFE_LOG2_RANGE)
    def _():
        l_ref[...] = jnp.zeros_like(l_ref)
        acc_ref[...] = jnp.zeros_like(acc_ref)

        def step(i, carry):
            s, v = scores(i)
            p = jnp.exp2(s - bound)
            part = p[:, 0:LANES]
            for r in range(1, reps):
                part = part + p[:, r * LANES:(r + 1) * LANES]
            l_ref[...] += part
            acc_ref[...] += _dot(p.astype(BF16), v)
            return carry

        lax.fori_loop(0, nk, step, 0)
        finalize(acc_ref[...] / jnp.sum(l_ref[...], axis=1, keepdims=True))

    @pl.when(bound > FLASH_SAFE_LOG2_RANGE)
    def _():
        m_ref[...] = jnp.full_like(m_ref, -jnp.inf)
        l_ref[...] = jnp.zeros_like(l_ref)
        acc_ref[...] = jnp.zeros_like(acc_ref)

        def step(i, carry):
            s, v = scores(i)
            m_prev = m_ref[...]
            m_new = jnp.maximum(m_prev, jnp.max(s, axis=1, keepdims=True))
            alpha = jnp.exp2(m_prev - m_new)
            p = jnp.exp2(s - jnp.concatenate([m_new] * reps, axis=1))
            l_ref[...] = alpha * l_ref[...] + jnp.sum(p, axis=1, keepdims=True)
            acc_ref[...] = alpha * acc_ref[...] + _dot(p.astype(BF16), v)
            m_ref[...] = m_new
            return carry

        lax.fori_loop(0, nk, step, 0)
        finalize(acc_ref[...] / l_ref[...])


def _rope_tables(t):
    half = DIFF_DH // 2
    inv_freq = ROPE_THETA ** (-jnp.arange(0, DIFF_DH, 2, dtype=F32) / DIFF_DH)
    ang = jnp.arange(t, dtype=F32)[:, None] * inv_freq[None, :]
    cos, sin = jnp.cos(ang), jnp.sin(ang)
    cos_t = jnp.tile(cos, (1, LANES // half))
    sin_t = jnp.tile(jnp.concatenate([-sin, sin], axis=1), (1, LANES // DIFF_DH))
    return cos_t, sin_t


def _diff_branch(proj, g_q, g_k, lq1, lk1, lq2, lk2, g_sub, lam_init, tm, tq, tk):
    bsz, t, _ = proj.shape
    tm, tq, tk = min(tm, t), min(tq, t), min(tk, t)
    cos_t, sin_t = _rope_tables(t)
    tile_g = lambda g: jnp.tile(g.astype(F32), DIFF_QK // DIFF_DH).reshape(1, DIFF_QK)
    qz, kt, vh = pl.pallas_call(
        _diff_prep_kernel,
        grid=(bsz, t // tm),
        in_specs=[pl.BlockSpec((1, tm, DIFF_QK), lambda b, i: (b, i, OFF_DQ // DIFF_QK)),
                  pl.BlockSpec((1, tm, DIFF_QK), lambda b, i: (b, i, OFF_DK // DIFF_QK)),
                  pl.BlockSpec((1, tm, DIFF_V), lambda b, i: (b, i, OFF_DV // DIFF_V)),
                  pl.BlockSpec((1, DIFF_QK), lambda b, i: (0, 0)),
                  pl.BlockSpec((1, DIFF_QK), lambda b, i: (0, 0)),
                  pl.BlockSpec((tm, LANES), lambda b, i: (i, 0)),
                  pl.BlockSpec((tm, LANES), lambda b, i: (i, 0))],
        out_specs=[pl.BlockSpec((1, tm, 2 * DIFF_QK), lambda b, i: (b, i, 0)),
                   pl.BlockSpec((1, DIFF_HEADS, LANES, tm), lambda b, i: (b, 0, 0, i)),
                   pl.BlockSpec((1, DIFF_HEADS, tm, DIFF_DV), lambda b, i: (b, 0, i, 0))],
        out_shape=[jax.ShapeDtypeStruct((bsz, t, 2 * DIFF_QK), BF16),
                   jax.ShapeDtypeStruct((bsz, DIFF_HEADS, LANES, t), BF16),
                   jax.ShapeDtypeStruct((bsz, DIFF_HEADS, t, DIFF_DV), BF16)],
        compiler_params=_params("parallel", "parallel"),
    )(proj, proj, proj, tile_g(g_q), tile_g(g_k), cos_t, sin_t)

    vec = lambda a: a.reshape(1, DIFF_DH).astype(F32)
    bound = (1.01 * DIFF_DH * (DIFF_DH ** -0.5 * LOG2E)
             * jnp.max(jnp.abs(g_q)) * jnp.max(jnp.abs(g_k))).astype(F32).reshape(1)
    return pl.pallas_call(
        functools.partial(_flash_kernel, tq=tq, tk=tk, nk=t // tk, lam_init=lam_init),
        grid=(bsz, DIFF_HEADS, t // tq),
        in_specs=[pl.BlockSpec(memory_space=pltpu.SMEM),
                  pl.BlockSpec((1, tq, 2 * LANES), lambda b, h, i: (b, i, h)),
                  pl.BlockSpec((1, 1, LANES, t), lambda b, h, i: (b, h, 0, 0)),
                  pl.BlockSpec((1, 1, t, DIFF_DV), lambda b, h, i: (b, h, 0, 0)),
                  pl.BlockSpec((1, DIFF_DH), lambda b, h, i: (0, 0)),
                  pl.BlockSpec((1, DIFF_DH), lambda b, h, i: (0, 0)),
                  pl.BlockSpec((1, DIFF_DH), lambda b, h, i: (0, 0)),
                  pl.BlockSpec((1, DIFF_DH), lambda b, h, i: (0, 0)),
                  pl.BlockSpec((1, DIFF_DV), lambda b, h, i: (0, 0))],
        out_specs=pl.BlockSpec((1, tq, DIFF_DV), lambda b, h, i: (b, i, h)),
        out_shape=jax.ShapeDtypeStruct((bsz, t, DIFF_V), BF16),
        scratch_shapes=[pltpu.VMEM((2 * tq, LANES), BF16),
                        pltpu.VMEM((2 * tq, LANES), F32),
                        pltpu.VMEM((2 * tq, LANES), F32),
                        pltpu.VMEM((2 * tq, DIFF_DV), F32)],
        compiler_params=_params("parallel", "parallel", "arbitrary"),
    )(bound, qz, kt, vh, vec(lq1), vec(lk1), vec(lq2), vec(lk2), g_sub.reshape(1, DIFF_DV).astype(F32))


def _mem_kernel(q_ref, kv_ref, gq_ref, gk_ref, o_ref):
    outs = []
    for h in range(MEM_HEADS):
        hs = slice(h * MEM_DH, (h + 1) * MEM_DH)
        q = q_ref[0, :, hs].astype(F32)
        q = q * lax.rsqrt(jnp.mean(q * q, axis=-1, keepdims=True) + RMS_EPS) * gq_ref[...]
        k = kv_ref[0, :, hs]
        k = k * lax.rsqrt(jnp.mean(k * k, axis=-1, keepdims=True) + RMS_EPS) * gk_ref[...]
        v = kv_ref[0, :, MEM_Q + h * MEM_DH:MEM_Q + (h + 1) * MEM_DH]
        s = _dot_nt(q.astype(BF16), k.astype(BF16)) * (MEM_DH ** -0.5)
        p = jnp.exp(s - jnp.max(s, axis=-1, keepdims=True))
        p = p / jnp.sum(p, axis=-1, keepdims=True)
        outs.append(_dot(p.astype(BF16), v.astype(BF16)))
    o_ref[0] = jnp.concatenate(outs, axis=1).astype(o_ref.dtype)


def _mem_branch(proj, kv, g_q, g_k, tm):
    bsz, t, _ = proj.shape
    tm = min(tm, t)
    return pl.pallas_call(
        _mem_kernel,
        grid=(bsz, t // tm),
        in_specs=[pl.BlockSpec((1, tm, MEM_Q), lambda b, i: (b, i, OFF_MQ // MEM_Q)),
                  pl.BlockSpec((1, N_MEM, 2 * MEM_Q), lambda b, i: (b, 0, 0)),
                  pl.BlockSpec((1, MEM_DH), lambda b, i: (0, 0)),
                  pl.BlockSpec((1, MEM_DH), lambda b, i: (0, 0))],
        out_specs=pl.BlockSpec((1, tm, MEM_Q), lambda b, i: (b, i, 0)),
        out_shape=jax.ShapeDtypeStruct((bsz, t, MEM_Q), BF16),
        compiler_params=_params("parallel", "parallel"),
    )(proj, kv, g_q.reshape(1, MEM_DH).astype(F32), g_k.reshape(1, MEM_DH).astype(F32))


def _merge_kernel(x_ref, pg_ref, pd_ref, pm_ref, g0_ref, g1_ref, g2_ref,
                  wg_ref, wd_ref, wm_ref, wo_ref, o_ref):
    merged = (jax.nn.sigmoid(g0_ref[...].astype(F32)) * _dot(pg_ref[...], wg_ref[...])
              + jax.nn.sigmoid(g1_ref[...].astype(F32)) * _dot(pd_ref[...], wd_ref[...])
              + jax.nn.sigmoid(g2_ref[...].astype(F32)) * _dot(pm_ref[...], wm_ref[...]))
    o_ref[...] = x_ref[...] + _dot(merged.astype(BF16), wo_ref[...])


def _merge(x, proj, pg, pd, pm, wg, wd, wm, wo, tm):
    n, d = x.shape
    tm = min(tm, n)
    row = lambda c: pl.BlockSpec((tm, d), lambda i: (i, c))
    full = pl.BlockSpec((d, d), lambda i: (0, 0))
    br = OFF_BR // d
    return pl.pallas_call(
        _merge_kernel,
        grid=(n // tm,),
        in_specs=[row(0), row(0), row(0), row(0), row(br), row(br + 1), row(br + 2), full, full, full, full],
        out_specs=row(0),
        out_shape=jax.ShapeDtypeStruct((n, d), F32),
        compiler_params=_params("parallel"),
    )(x, pg, pd, pm, proj, proj, proj, wg, wd, wm, wo)


PEER_ECHUNK = 2048
PEER_IPER = PEER_ECHUNK // PEER_NKEYS
PEER_PIECE = 2


def _bitonic_merge_desc(a):
    n = len(a)
    j = n // 2
    while j >= 1:
        for i in range(n):
            l = i ^ j
            if l > i:
                a[i], a[l] = jnp.maximum(a[i], a[l]), jnp.minimum(a[i], a[l])
        j //= 2
    return a


def _top16_desc(vals):
    a = list(vals)
    n = len(a)
    k = 2
    while k <= n:
        j = k // 2
        while j >= 1:
            for i in range(n):
                l = i ^ j
                if l > i:
                    hi, lo = jnp.maximum(a[i], a[l]), jnp.minimum(a[i], a[l])
                    a[i], a[l] = (hi, lo) if (i & k) == 0 else (lo, hi)
            j //= 2
        k *= 2
    for shift in (4, 2, 1):
        a = [jnp.maximum(a[r], pltpu.roll(a[n - 1 - r], shift, axis=0)) for r in range(n)]
        a = _bitonic_merge_desc(a)
    return a


def _dup_bf16_words(x):
    bits = lax.bitcast_convert_type(x.astype(BF16).astype(F32), jnp.uint32)
    return lax.bitcast_convert_type(bits | (bits >> 16), F32)


def _packed_row(row, nrows):
    return pltpu.bitcast(jnp.broadcast_to(row, (nrows // 2, row.shape[1])), BF16)


def _peer_kernel(x_ref, g_ref, wqt_ref, k1h_ref, k1l_ref, k2h_ref, k2l_ref, u_ref, vt_ref, o_ref,
                 hbt_ref, t_ref, c_ref, e2_ref, a_ref, wt_ref, acc_ref):
    ec = pl.program_id(1)
    tt = x_ref.shape[0]
    neg = jnp.full((SUBLANES, tt), -jnp.inf, F32)
    sub = lax.broadcasted_iota(jnp.int32, (SUBLANES, tt), 0)

    @pl.when(ec == 0)
    def _():
        x = x_ref[...]
        hb = x * lax.rsqrt(jnp.mean(x * x, axis=-1, keepdims=True) + RMS_EPS) * g_ref[...]
        hbt_ref[...] = hb.T.astype(BF16)
        acc_ref[...] = jnp.zeros_like(acc_ref)
        for h in range(PEER_HEADS):
            def scores(kh_ref, kl_ref, row0):
                qt = _dot(wqt_ref[row0:row0 + PEER_DQH, :], hbt_ref[...])
                qh = qt.astype(BF16)
                ql = (qt - qh.astype(F32)).astype(BF16)
                return _dot(kh_ref[...], qh) + _dot(kh_ref[...], ql) + _dot(kl_ref[...], qh)

            s1 = scores(k1h_ref, k1l_ref, h * PEER_DQ)
            s2 = scores(k2h_ref, k2l_ref, h * PEER_DQ + PEER_DQH)
            blocks = lambda s: [s[r * SUBLANES:(r + 1) * SUBLANES, :] for r in range(PEER_NKEYS // SUBLANES)]
            v1 = _top16_desc(blocks(s1))
            v2 = _top16_desc(blocks(s2))

            def pack(vs):
                out = neg
                for r in range(SUBLANES):
                    out = jnp.where(sub == r, vs[r], out)
                return out

            v2_lo, v2_hi, v1_hi = pack(v2[:SUBLANES]), pack(v2[SUBLANES:]), pack(v1[SUBLANES:])
            cand = [v1[0] + v2_lo, v1[0] + v2_hi, v1_hi + v2[0]]
            cand += [v1[a] + v2_lo for a in range(1, SUBLANES)]
            cand += [neg] * (PEER_TOPK - len(cand))
            top = _top16_desc(cand)
            tau16 = top[PEER_TOPK - 1]
            below = lambda s, bound: jnp.max(jnp.where(s < bound, s, -jnp.inf), axis=0, keepdims=True)
            tau17 = below(jnp.concatenate(cand[:10], axis=0), tau16[0:1, :])
            tau17 = jnp.maximum(tau17, below(s1, v1[PEER_TOPK - 1][0:1, :]) + v2[0][0:1, :])
            tau17 = jnp.maximum(tau17, below(s2, v2[PEER_TOPK - 1][0:1, :]) + v1[0][0:1, :])
            tau = 0.5 * (tau16[0:1, :] + tau17)
            z = top[0] - top[0]
            for r in range(PEER_TOPK):
                z = z + jnp.exp(top[r] - top[0])
            z = z[0:1, :]
            nchunks = PEER_NKEYS // PEER_IPER
            m2 = v2[0][0:1, :]
            t_ref[h] = _dup_bf16_words(jnp.exp(tau - s1 - m2)).reshape(nchunks, PEER_IPER, tt)
            c_ref[h] = _dup_bf16_words(0.5 * jnp.exp(s1 - v1[0][0:1, :]) / z).reshape(nchunks, PEER_IPER, tt)
            e2_ref[h] = jnp.exp(s2 - m2).astype(BF16)

    half_lanes = tt // 2

    def gates(il, ls):
        acc = None
        for h in range(PEER_HEADS):
            e2v = e2_ref[h, :, ls]
            thr = _packed_row(t_ref[h, ec, il:il + 1, ls], PEER_NKEYS)
            coef = _packed_row(c_ref[h, ec, il:il + 1, ls], PEER_NKEYS)
            term = jnp.where(e2v >= thr, e2v, jnp.zeros_like(e2v)) * coef
            acc = term if acc is None else acc + term
        return acc

    def weights(il, ls, g):
        rs = slice(il * PEER_NKEYS, (il + 1) * PEER_NKEYS)
        a = a_ref[rs, ls]
        gelu2 = a * (1.0 + lax.erf(a * (2.0 ** -0.5)))
        wt_ref[rs, ls] = gelu2.astype(BF16) * g

    def value_matmul(hs):
        acc_ref[:, hs] += _dot(vt_ref[0], wt_ref[:, hs])

    lane_blocks = [slice(lb * LANES, (lb + 1) * LANES) for lb in range(tt // LANES)]
    first, second = lane_blocks[:len(lane_blocks) // 2], lane_blocks[len(lane_blocks) // 2:]
    for lanes, hs in ((first, slice(0, half_lanes)), (second, slice(half_lanes, tt))):
        for piece in range(PEER_IPER // PEER_PIECE):
            if lanes is first:
                ps = slice(piece * PEER_PIECE * PEER_NKEYS, (piece + 1) * PEER_PIECE * PEER_NKEYS)
                a_ref[ps, :] = _dot(u_ref[ps, :], hbt_ref[...])
            for il in range(piece * PEER_PIECE, (piece + 1) * PEER_PIECE):
                for ls in lanes:
                    weights(il, ls, gates(il, ls))
        value_matmul(hs)

    @pl.when(ec == pl.num_programs(1) - 1)
    def _():
        o_ref[...] = x_ref[...] + acc_ref[...].T


def _peer(x, g_ffn, wqt, k1h, k1l, k2h, k2l, u, vt, tt):
    n, d = x.shape
    tt = min(tt, n)
    nec = PEER_EXPERTS // PEER_ECHUNK
    const = lambda shape: pl.BlockSpec(shape, lambda i, e: (0,) * len(shape))
    key_spec = const((PEER_NKEYS, PEER_DQH))
    assert tt % (2 * LANES) == 0 and n % tt == 0
    stat_i = pltpu.VMEM((PEER_HEADS, PEER_NKEYS // PEER_IPER, PEER_IPER, tt), F32)
    return pl.pallas_call(
        _peer_kernel,
        grid=(n // tt, nec),
        in_specs=[pl.BlockSpec((tt, d), lambda i, e: (i, 0)),
                  const((1, d)),
                  const((PEER_HEADS * PEER_DQ, d)),
                  key_spec, key_spec, key_spec, key_spec,
                  pl.BlockSpec((PEER_ECHUNK, d), lambda i, e: (e, 0)),
                  pl.BlockSpec((1, d, PEER_ECHUNK), lambda i, e: (e, 0, 0))],
        out_specs=pl.BlockSpec((tt, d), lambda i, e: (i, 0)),
        out_shape=jax.ShapeDtypeStruct((n, d), F32),
        scratch_shapes=[pltpu.VMEM((d, tt), BF16), stat_i, stat_i,
                        pltpu.VMEM((PEER_HEADS, PEER_NKEYS, tt), BF16),
                        pltpu.VMEM((PEER_ECHUNK, tt), F32),
                        pltpu.VMEM((PEER_ECHUNK, tt), BF16),
                        pltpu.VMEM((d, tt), F32)],
        compiler_params=_params("parallel", "arbitrary"),
    )(x, g_ffn.reshape(1, d), wqt, k1h, k1l, k2h, k2l, u, vt)


def _split_hi_lo(a):
    hi = a.astype(BF16)
    return hi, (a - hi.astype(F32)).astype(BF16)


def _prepare_weights(g_mix, w_in, gla_w_dec_f, gla_b_dec_f, gla_w_dec_b, gla_b_dec_b, gla_g_out, gla_w_o,
                     diff_g_q, diff_g_k, diff_lq1, diff_lk1, diff_lq2, diff_lk2, diff_g_sub, diff_w_o,
                     mem_g_norm, mem_w_kv, mem_g_q, mem_g_k, mem_w_o, w_out,
                     g_ffn, peer_w_q, peer_sub_k1, peer_sub_k2, peer_u, peer_v):
    w = w_in[0]
    lr0 = 2 * GLA_QK + 2 * GLA_V
    lr1 = lr0 + 2 * GLA_LOWRANK
    w_main = jnp.concatenate([w[:, :lr0], w[:, lr1:]], axis=1).astype(BF16)
    w_lr = jnp.pad(w[:, lr0:lr1], ((0, 0), (0, LANES - 2 * GLA_LOWRANK))).astype(BF16)
    k1h, k1l = _split_hi_lo(peer_sub_k1[0])
    k2h, k2l = _split_hi_lo(peer_sub_k2[0])
    return dict(
        g_mix=g_mix[0], w_main=w_main, w_lr=w_lr,
        gla=(gla_w_dec_f[0], gla_b_dec_f[0], gla_w_dec_b[0], gla_b_dec_b[0], gla_g_out[0]),
        diff=(diff_g_q[0], diff_g_k[0], diff_lq1[0], diff_lk1[0], diff_lq2[0], diff_lk2[0], diff_g_sub[0]),
        mem_g_norm=mem_g_norm[0], mem_w_kv=mem_w_kv[0].astype(BF16), mem_gq=mem_g_q[0], mem_gk=mem_g_k[0],
        w_o=(gla_w_o[0].astype(BF16), diff_w_o[0].astype(BF16), mem_w_o[0].astype(BF16), w_out[0].astype(BF16)),
        g_ffn=g_ffn[0], wqt=peer_w_q[0].T.astype(BF16), keys=(k1h, k1l, k2h, k2l),
        u=peer_u[0].astype(BF16),
        vt=peer_v[0].astype(BF16).reshape(PEER_EXPERTS // PEER_ECHUNK, PEER_ECHUNK, D_MODEL).transpose(0, 2, 1),
    )


def _trunk(x, mem, p):
    bsz, t, d = x.shape
    n = bsz * t
    lam_init = 0.8 - 0.6 * math.exp(-0.3 * 0)
    x2 = x.reshape(n, d)
    proj = _rms_matmul(x2, p["g_mix"], p["w_main"], BF16, 1024, 2048).reshape(bsz, t, MAIN_COLS)
    lr = _rms_matmul(x2, p["g_mix"], p["w_lr"], F32, 1024, LANES).reshape(bsz, t, LANES)
    kv = _rms_matmul(mem.reshape(bsz * N_MEM, d), p["mem_g_norm"], p["mem_w_kv"], F32, 512, 2 * MEM_Q)
    kv = kv.reshape(bsz, N_MEM, 2 * MEM_Q)
    pre_gla = _gla_branch(proj, lr, *p["gla"], tb=512)
    pre_diff = _diff_branch(proj, *p["diff"], lam_init=lam_init, tm=512, tq=1024, tk=1024)
    pre_mem = _mem_branch(proj, kv, p["mem_gq"], p["mem_gk"], tm=512)
    x1 = _merge(x2, proj.reshape(n, MAIN_COLS), pre_gla.reshape(n, d), pre_diff.reshape(n, d),
                pre_mem.reshape(n, d), *p["w_o"], tm=512)
    y = _peer(x1, p["g_ffn"], p["wqt"], *p["keys"], p["u"], p["vt"], tt=512)
    return y.reshape(bsz, t, d)


def kernel(x_prompt, x_sample, mem_prompt, mem_sample, g_mix, w_in, gla_w_dec_f, gla_b_dec_f, gla_w_dec_b, gla_b_dec_b, gla_g_out, gla_w_o, diff_g_q, diff_g_k, diff_lq1, diff_lk1, diff_lq2, diff_lk2, diff_g_sub, diff_w_o, mem_g_norm, mem_w_kv, mem_g_q, mem_g_k, mem_w_o, w_out, g_ffn, peer_w_q, peer_sub_k1, peer_sub_k2, peer_u, peer_v):
    p = _prepare_weights(g_mix, w_in, gla_w_dec_f, gla_b_dec_f, gla_w_dec_b, gla_b_dec_b, gla_g_out, gla_w_o,
                         diff_g_q, diff_g_k, diff_lq1, diff_lk1, diff_lq2, diff_lk2, diff_g_sub, diff_w_o,
                         mem_g_norm, mem_w_kv, mem_g_q, mem_g_k, mem_w_o, w_out,
                         g_ffn, peer_w_q, peer_sub_k1, peer_sub_k2, peer_u, peer_v)
    return (_trunk(x_prompt, mem_prompt, p), _trunk(x_sample, mem_sample, p))
```
